```python
import jax, jax.numpy as jnp
from jax import lax
import numpy as np

D_MODEL = 1024
BATCH = 16
SEQ = 4096
DEPTH = 1

D_MIX = D_MODEL
A_HEADS = 8
A_DK = 64
A_DV = 64
A_KWIDTH = A_HEADS * A_DK
A_WIDTH = A_HEADS * A_DV
CHUNK = 64
B_HEADS = 4
B_NOPE = 128
B_ROPE = 64
B_V = 128
B_WIDTH = B_HEADS * B_V
Q_LORA = 384
KV_LORA = 256
ROPE_THETA = 10000.0
Q_BLOCK = 128
D_FF = ((8 * D_MODEL + 3 * 256 - 1) // (3 * 256)) * 256
EPS = 1e-6
IN_SPLITS = (A_KWIDTH, A_WIDTH, A_KWIDTH, A_KWIDTH, A_WIDTH, Q_LORA, KV_LORA, B_ROPE)
D_IN = A_KWIDTH * 3 + A_WIDTH * 2 + Q_LORA + KV_LORA + B_ROPE

kernel_name = 'hybrid_hgrn2_mla_encoder_block'


def _rmsnorm(x, g):
    xf = x.astype(jnp.float32)
    y = xf * lax.rsqrt(jnp.mean(xf * xf, axis=-1, keepdims=True) + EPS)
    return (y * g.astype(jnp.float32)).astype(x.dtype)


def _rope_tables(seq):
    inv = 1.0 / (ROPE_THETA ** (jnp.arange(0, B_ROPE, 2, dtype=jnp.float32) / B_ROPE))
    ang = jnp.arange(seq, dtype=jnp.float32)[:, None] * inv[None, :]
    return jnp.cos(ang), jnp.sin(ang)


def _apply_rope(x, cos, sin):
    xf = x.astype(jnp.float32)
    x1, x2 = jnp.split(xf, 2, axis=-1)
    out = jnp.concatenate([x1 * cos - x2 * sin, x1 * sin + x2 * cos], axis=-1)
    return out.astype(x.dtype)


def _gla_chunkwise(q, k, v, log_f):
    bsz, nh, seq, dk = q.shape
    dv = v.shape[-1]
    n = seq // CHUNK
    q = q.reshape(bsz, nh, n, CHUNK, dk)
    k = k.reshape(bsz, nh, n, CHUNK, dk)
    log_f = log_f.reshape(bsz, nh, n, CHUNK, dk)
    v = v.reshape(bsz, nh, n, CHUNK, dv)
    cum = jnp.cumsum(log_f, axis=3)
    last = cum[:, :, :, -1:, :]
    q_dec = q * jnp.exp(cum)
    k_inv = k * jnp.exp(-cum)
    k_to_end = k * jnp.exp(last - cum)
    mask = jnp.tril(jnp.ones((CHUNK, CHUNK), dtype=bool))
    scores = jnp.einsum('bhnid,bhnjd->bhnij', q_dec, k_inv)
    o_intra = jnp.einsum('bhnij,bhnje->bhnie', jnp.where(mask, scores, 0.0), v)
    u = jnp.einsum('bhnjd,bhnje->bhnde', k_to_end, v)
    decay = jnp.exp(last[:, :, :, 0, :])

    def step(s, xs):
        d, u_n = xs
        return d[..., None] * s + u_n, s

    s0 = jnp.zeros((bsz, nh, dk, dv), dtype=q.dtype)
    _, s_prev = lax.scan(step, s0, (jnp.moveaxis(decay, 2, 0), jnp.moveaxis(u, 2, 0)))
    s_prev = jnp.moveaxis(s_prev, 0, 2)
    o_inter = jnp.einsum('bhnid,bhnde->bhnie', q_dec, s_prev)
    return (o_intra + o_inter).reshape(bsz, nh, seq, dv)


def _hgrn2_group(hq, hi, hf_fwd, hf_bwd, hg, lb, norm_g):
    bsz, seq, _ = hq.shape
    f32 = jnp.float32

    def to_heads(a, d):
        return a.astype(f32).reshape(bsz, seq, A_HEADS, d).transpose(0, 2, 1, 3)

    q = to_heads(jax.nn.silu(hq), A_DK)
    v = to_heads(hi, A_DV)

    def direction(pre, lb_dir, reverse):
        lb_h = lb_dir.astype(f32).reshape(A_HEADS, 1, A_DK)
        z = to_heads(pre, A_DK)
        log_f = jnp.log(lb_h + (1.0 - lb_h) * jax.nn.sigmoid(z))
        k = (1.0 - lb_h) * jax.nn.sigmoid(-z)
        if reverse:
            o = _gla_chunkwise(jnp.flip(q, 2), jnp.flip(k, 2), jnp.flip(v, 2), jnp.flip(log_f, 2))
            return jnp.flip(o, 2)
        return _gla_chunkwise(q, k, v, log_f)

    o = direction(hf_fwd, lb[0], False) + direction(hf_bwd, lb[1], True)
    o = o.transpose(0, 2, 1, 3)
    o = o * lax.rsqrt(jnp.mean(o * o, axis=-1, keepdims=True) + EPS)
    o = o * norm_g.astype(f32).reshape(A_HEADS, A_DV)
    o = o.reshape(bsz, seq, A_WIDTH) * jax.nn.silu(hg.astype(f32))
    return o.astype(hq.dtype)


def _mla_group(c_q, c_kv, k_rope, g_qa, w_qb, g_kva, w_kvb, g_out):
    bsz, seq, _ = c_q.shape
    cos, sin = _rope_tables(seq)
    q = (_rmsnorm(c_q, g_qa) @ w_qb).reshape(bsz, seq, B_HEADS, B_NOPE + B_ROPE)
    q_nope, q_rope = q[..., :B_NOPE], q[..., B_NOPE:]
    kv = (_rmsnorm(c_kv, g_kva) @ w_kvb).reshape(bsz, seq, B_HEADS, B_NOPE + B_V)
    k_nope, v = kv[..., :B_NOPE], kv[..., B_NOPE:]
    q_rope = _apply_rope(q_rope, cos[:, None, :], sin[:, None, :])
    k_rope = _apply_rope(k_rope, cos, sin)
    scale = (B_NOPE + B_ROPE) ** -0.5
    n_blk = seq // Q_BLOCK
    qn_blocks = q_nope.reshape(bsz, n_blk, Q_BLOCK, B_HEADS, B_NOPE).swapaxes(0, 1)
    qr_blocks = q_rope.reshape(bsz, n_blk, Q_BLOCK, B_HEADS, B_ROPE).swapaxes(0, 1)

    def attend(blk):
        qn, qr = blk
        s = (jnp.einsum('bqhd,bkhd->bhqk', qn, k_nope)
             + jnp.einsum('bqhr,bkr->bhqk', qr, k_rope))
        p = jax.nn.softmax(s.astype(jnp.float32) * scale, axis=-1)
        return jnp.einsum('bhqk,bkhe->bqhe', p.astype(v.dtype), v)

    o = lax.map(attend, (qn_blocks, qr_blocks))
    o = o.swapaxes(0, 1).reshape(bsz, seq, B_WIDTH)
    return _rmsnorm(o, g_out)


def setup_inputs(seed: int = 0) -> dict:
    key = jax.random.key(seed)
    ks = jax.random.split(key, 20)
    f32 = jnp.float32

    def nrm(k, shape, fan_in):
        return jax.random.normal(k, shape, f32) * (fan_in ** -0.5)

    def gain(k, shape):
        return 1.0 + 0.02 * jax.random.normal(k, shape, f32)

    return {
        'x': jax.random.normal(ks[0], (BATCH, SEQ, D_MODEL), f32),
        'norm1_g': gain(ks[1], (DEPTH, D_MODEL)),
        'w_in': nrm(ks[2], (DEPTH, D_MODEL, D_IN), D_MODEL),
        'lb_logits': 0.1 * jax.random.normal(ks[3], (2, DEPTH + 1, A_KWIDTH), f32),
        'hgrn_norm_g': gain(ks[4], (DEPTH, A_WIDTH)),
        'q_a_norm_g': gain(ks[5], (DEPTH, Q_LORA)),
        'w_q_b': nrm(ks[6], (DEPTH, Q_LORA, B_HEADS * (B_NOPE + B_ROPE)), Q_LORA),
        'kv_a_norm_g': gain(ks[7], (DEPTH, KV_LORA)),
        'w_kv_b': nrm(ks[8], (DEPTH, KV_LORA, B_HEADS * (B_NOPE + B_V)), KV_LORA),
        'mla_norm_g': gain(ks[9], (DEPTH, B_WIDTH)),
        'w_out': nrm(ks[10], (DEPTH, D_MIX, D_MODEL), D_MIX),
        'norm2_g': gain(ks[11], (DEPTH, D_MODEL)),
        'w_gate': nrm(ks[12], (DEPTH, D_MODEL, D_FF), D_MODEL),
        'w_up': nrm(ks[13], (DEPTH, D_MODEL, D_FF), D_MODEL),
        'w_down': nrm(ks[14], (DEPTH, D_FF, D_MODEL), D_FF),
        'final_norm_g': gain(ks[15], (D_MODEL,)),
    }


def reference(x, norm1_g, w_in, lb_logits, hgrn_norm_g, q_a_norm_g, w_q_b, kv_a_norm_g,
              w_kv_b, mla_norm_g, w_out, norm2_g, w_gate, w_up, w_down, final_norm_g):
    p = jax.nn.softmax(lb_logits.astype(jnp.float32), axis=1)
    lower_bounds = jnp.cumsum(p, axis=1)[:, :DEPTH]
    split_at = [int(v) for v in np.cumsum(IN_SPLITS)[:-1]]
    for l in range(DEPTH):
        h = _rmsnorm(x, norm1_g[l])
        proj = h @ w_in[l]
        hq, hi, hf_fwd, hf_bwd, hg, c_q, c_kv, k_r = jnp.split(proj, split_at, axis=-1)
        y_a = _hgrn2_group(hq, hi, hf_fwd, hf_bwd, hg, lower_bounds[:, l], hgrn_norm_g[l])
        y_b = _mla_group(c_q, c_kv, k_r, q_a_norm_g[l], w_q_b[l], kv_a_norm_g[l],
                         w_kv_b[l], mla_norm_g[l])
        x = x + jnp.concatenate([y_a, y_b], axis=-1) @ w_out[l]
        h = _rmsnorm(x, norm2_g[l])
        x = x + (jax.nn.silu(h @ w_gate[l]) * (h @ w_up[l])) @ w_down[l]
    return _rmsnorm(x, final_norm_g)
```

```python
import functools

import jax
import jax.numpy as jnp
from jax import lax
from jax.experimental import pallas as pl
from jax.experimental.pallas import tpu as pltpu

EPS = 1e-6
ROPE_THETA = 10000.0
HGRN_HEADS = 8
HGRN_DK = 64
HGRN_WIDTH = HGRN_HEADS * HGRN_DK
HGRN_CHUNK = 64
PAIR = 2 * HGRN_DK
MLA_HEADS = 4
MLA_NOPE = 128
MLA_ROPE = 64
MLA_V = 128
MLA_QK = MLA_NOPE + MLA_ROPE
MLA_WIDTH = MLA_HEADS * MLA_V
Q_LORA = 384
KV_LORA = 256
LANES = 128

BF16 = jnp.bfloat16
F32 = jnp.float32

_NT = (((1,), (1,)), ((), ()))
_TN = (((0,), (0,)), ((), ()))


def _dot(a, b):
    return jnp.dot(a, b, preferred_element_type=F32)


def _dot_nt(a, b):
    return lax.dot_general(a, b, _NT, preferred_element_type=F32)


def _dot_tn(a, b):
    return lax.dot_general(a, b, _TN, preferred_element_type=F32)


def _rms(x, g):
    return x * lax.rsqrt(jnp.mean(x * x, axis=-1, keepdims=True) + EPS) * g


def _sigmoid(x):
    return 1.0 / (1.0 + jnp.exp(-x))


def _split_bf16(x):
    hi = x.astype(BF16)
    lo = (x - hi.astype(F32)).astype(BF16)
    return hi, lo


def _in_proj_kernel(x_ref, g1_ref, wh_ref, wc_ref, gq_ref, gkv_ref, wqt_ref, wkn_ref, wvt_ref,
                    cos2_ref, sin2_ref, cost_ref, sint_ref,
                    hq_ref, hi_ref, hf_ref, hb_ref, hg_ref, qt_ref, kc_ref, vt_ref, *, scale):
    hn = _rms(x_ref[0], g1_ref[...]).astype(BF16)
    for g, o_ref in enumerate((hq_ref, hi_ref, hf_ref, hb_ref, hg_ref)):
        w = wh_ref[:, g * HGRN_WIDTH:(g + 1) * HGRN_WIDTH]
        o_ref[0] = _dot(hn, w).astype(o_ref.dtype)

    c = _dot(hn, wc_ref[...])
    c_q = c[:, :Q_LORA]
    c_kv = c[:, Q_LORA:Q_LORA + KV_LORA]
    k_r = c[:, 640:640 + MLA_ROPE]
    k_rot = c[:, 768:768 + MLA_ROPE]
    cqn = _rms(c_q, gq_ref[...]).astype(BF16)
    ckvn = _rms(c_kv, gkv_ref[...]).astype(BF16)

    qt = _dot_nt(wqt_ref[...], cqn)
    cos_t = cost_ref[...]
    sin_t = sint_ref[...]
    half = MLA_ROPE // 2
    for h in range(MLA_HEADS):
        r0 = h * MLA_QK
        x1 = qt[r0 + MLA_NOPE:r0 + MLA_NOPE + half]
        x2 = qt[r0 + MLA_NOPE + half:r0 + MLA_QK]
        qt_ref[0, r0:r0 + MLA_NOPE, :] = (qt[r0:r0 + MLA_NOPE] * scale).astype(BF16)
        qt_ref[0, r0 + MLA_NOPE:r0 + MLA_NOPE + half, :] = (
            (x1 * cos_t - x2 * sin_t) * scale).astype(BF16)
        qt_ref[0, r0 + MLA_NOPE + half:r0 + MLA_QK, :] = (
            (x1 * sin_t + x2 * cos_t) * scale).astype(BF16)

    k_rope = (k_r * cos2_ref[...] + k_rot * sin2_ref[...]).astype(BF16)
    k_nope = _dot(ckvn, wkn_ref[...])
    for h in range(MLA_HEADS):
        kc_ref[0, h, :, :MLA_NOPE] = k_nope[:, h * MLA_NOPE:(h + 1) * MLA_NOPE].astype(BF16)
        kc_ref[0, h, :, MLA_NOPE:] = k_rope
    vt_ref[0] = _dot_nt(wvt_ref[...], ckvn).astype(BF16)


def _in_proj_call(x, g1, wh, wc, gq, gkv, wqt, wkn, wvt, cos2, sin2, cos_t, sin_t, *, tm, scale):
    bsz, seq, d = x.shape
    grid = (bsz, seq // tm)
    full = lambda a: pl.BlockSpec(a.shape, lambda b, i: (0,) * a.ndim)
    tok = lambda w: pl.BlockSpec((1, tm, w), lambda b, i: (b, i, 0))
    in_specs = [
        tok(d), full(g1), full(wh), full(wc), full(gq), full(gkv), full(wqt), full(wkn), full(wvt),
        pl.BlockSpec((tm, MLA_ROPE), lambda b, i: (i, 0)),
        pl.BlockSpec((tm, MLA_ROPE), lambda b, i: (i, 0)),
        pl.BlockSpec((MLA_ROPE // 2, tm), lambda b, i: (0, i)),
        pl.BlockSpec((MLA_ROPE // 2, tm), lambda b, i: (0, i)),
    ]
    hshape = jax.ShapeDtypeStruct((bsz, seq, HGRN_WIDTH), F32)
    out_shape = [hshape] * 5 + [
        jax.ShapeDtypeStruct((bsz, MLA_HEADS * MLA_QK, seq), BF16),
        jax.ShapeDtypeStruct((bsz, MLA_HEADS, seq, MLA_QK), BF16),
        jax.ShapeDtypeStruct((bsz, MLA_WIDTH, seq), BF16),
    ]
    out_specs = [tok(HGRN_WIDTH)] * 5 + [
        pl.BlockSpec((1, MLA_HEADS * MLA_QK, tm), lambda b, i: (b, 0, i)),
        pl.BlockSpec((1, MLA_HEADS, tm, MLA_QK), lambda b, i: (b, 0, i, 0)),
        pl.BlockSpec((1, MLA_WIDTH, tm), lambda b, i: (b, 0, i)),
    ]
    return pl.pallas_call(
        functools.partial(_in_proj_kernel, scale=scale),
        grid=grid, in_specs=in_specs, out_specs=out_specs, out_shape=out_shape,
        compiler_params=pltpu.CompilerParams(
            dimension_semantics=("parallel", "parallel"), vmem_limit_bytes=56 * 2**20),
        name="in_proj",
    )(x, g1, wh, wc, gq, gkv, wqt, wkn, wvt, cos2, sin2, cos_t, sin_t)


def _hgrn_kernel(hq_ref, hi_ref, hf_ref, hb_ref, hg_ref, lbl_ref, gn_ref, y_ref, oacc_ref, *, n_chunks):
    c = HGRN_CHUNK
    row = lax.broadcasted_iota(jnp.int32, (c, c), 0)
    col = lax.broadcasted_iota(jnp.int32, (c, c), 1)
    lane = lax.broadcasted_iota(jnp.int32, (c, PAIR), 1)
    srow = lax.broadcasted_iota(jnp.int32, (c, PAIR), 0)
    head0 = lane < HGRN_DK
    brow = lax.broadcasted_iota(jnp.int32, (PAIR, PAIR), 0)
    bcol = lax.broadcasted_iota(jnp.int32, (PAIR, PAIR), 1)
    same_head = (brow < HGRN_DK) == (bcol < HGRN_DK)
    head_mean = jnp.where(same_head, 1.0 / HGRN_DK, 0.0).astype(BF16)

    lbl = lbl_ref[...]

    def lower_bound(d):
        l0 = lbl[2 * d:2 * d + 1]
        l1 = lbl[2 * d + 1:2 * d + 2]
        m = jnp.maximum(l0, l1)
        e0 = jnp.exp(l0 - m)
        e1 = jnp.exp(l1 - m)
        return e0 / (e0 + e1)

    def chunk(n, st, gate_ref, lb, reverse):
        rows = pl.ds(pl.multiple_of(n * c, c), c)
        z = gate_ref[0, rows, :]
        q = hq_ref[0, rows, :]
        q = q * _sigmoid(q)
        v = hi_ref[0, rows, :]
        log_f = jnp.log(lb + (1.0 - lb) * _sigmoid(z))
        k = (1.0 - lb) * _sigmoid(-z)
        if reverse:
            tri = (col >= row).astype(BF16)
            keep = (jnp.where(head0, lane, lane - HGRN_DK) >= srow)
        else:
            tri = (col <= row).astype(BF16)
            keep = (jnp.where(head0, lane, lane - HGRN_DK) <= srow)
        lf_hi, lf_lo = _split_bf16(log_f)
        cum = _dot(tri, lf_hi) + _dot(tri, lf_lo)
        total = cum[0:1] if reverse else cum[c - 1:c]
        q_dec = (q * jnp.exp(cum)).astype(BF16)
        k_inv = k * jnp.exp(-cum)
        k_end = (k * jnp.exp(total - cum)).astype(BF16)
        zero = jnp.zeros_like(k_inv)
        kk = jnp.concatenate([jnp.where(head0, k_inv, zero), jnp.where(head0, zero, k_inv)],
                             axis=0).astype(BF16)
        vv = jnp.concatenate([jnp.where(head0, v, zero), jnp.where(head0, zero, v)],
                             axis=0).astype(BF16)
        scores = _dot_nt(q_dec, kk)
        p = jnp.where(keep, scores, 0.0).astype(BF16)
        o = _dot(p, vv) + _dot_nt(q_dec, st.astype(BF16))
        u_t = _dot_tn(v.astype(BF16), k_end)
        st = st * jnp.exp(total) + jnp.where(same_head, u_t, 0.0)
        return rows, o, st

    st0 = jnp.zeros((PAIR, PAIR), F32)

    lb_f = lower_bound(0)

    def fwd_body(n, st):
        rows, o, st = chunk(n, st, hf_ref, lb_f, False)
        oacc_ref[rows, :] = o
        return st

    lax.fori_loop(0, n_chunks, fwd_body, st0)

    lb_b = lower_bound(1)
    gn = gn_ref[...]

    def bwd_body(i, st):
        n = n_chunks - 1 - i
        rows, o, st = chunk(n, st, hb_ref, lb_b, True)
        o = o + oacc_ref[rows, :]
        o2_hi, o2_lo = _split_bf16(o * o)
        ms = _dot(o2_hi, head_mean) + _dot(o2_lo, head_mean)
        g = hg_ref[0, rows, :]
        y_ref[0, rows, :] = (o * lax.rsqrt(ms + EPS) * gn * (g * _sigmoid(g))).astype(y_ref.dtype)
        return st

    lax.fori_loop(0, n_chunks, bwd_body, st0)


def _hgrn_call(hq, hi, hf, hb, hg, lbl, gn):
    bsz, seq, _ = hq.shape
    grid = (bsz, HGRN_WIDTH // PAIR)
    blk = pl.BlockSpec((1, seq, PAIR), lambda b, p: (b, 0, p))
    return pl.pallas_call(
        functools.partial(_hgrn_kernel, n_chunks=seq // HGRN_CHUNK),
        grid=grid,
        in_specs=[blk, blk, blk, blk, blk,
                  pl.BlockSpec((4, PAIR), lambda b, p: (0, p)),
                  pl.BlockSpec((1, PAIR), lambda b, p: (0, p))],
        out_specs=blk,
        out_shape=jax.ShapeDtypeStruct((bsz, seq, HGRN_WIDTH), BF16),
        scratch_shapes=[pltpu.VMEM((seq, PAIR), F32)],
        compiler_params=pltpu.CompilerParams(
            dimension_semantics=("parallel", "parallel"), vmem_limit_bytes=56 * 2**20),
        name="hgrn2",
    )(hq, hi, hf, hb, hg, lbl, gn)


def _attn_kernel(qt_ref, kc_ref, vt_ref, o_ref, *, tk, n_kv):
    tq = qt_ref.shape[2]
    for h in range(MLA_HEADS):
        q_t = qt_ref[0, h * MLA_QK:(h + 1) * MLA_QK, :]

        def body(j, carry):
            m, l, acc = carry
            ks = pl.ds(pl.multiple_of(j * tk, tk), tk)
            s = _dot(kc_ref[0, h, ks, :], q_t)
            m_new = jnp.maximum(m, jnp.max(s, axis=0, keepdims=True))
            alpha = jnp.exp(m - m_new)
            p = jnp.exp(s - m_new)
            l = alpha * l + jnp.sum(p, axis=0, keepdims=True)
            acc = alpha * acc + _dot(vt_ref[0, h * MLA_V:(h + 1) * MLA_V, ks], p.astype(BF16))
            return m_new, l, acc

        init = (jnp.full((1, tq), -jnp.inf, F32), jnp.zeros((1, tq), F32),
                jnp.zeros((MLA_V, tq), F32))
        _, l, acc = lax.fori_loop(0, n_kv, body, init)
        o_ref[0, :, h * MLA_V:(h + 1) * MLA_V] = (acc / l).T


def _attn_call(qt, kc, vt, *, tq, tk):
    bsz, _, seq = qt.shape
    grid = (bsz, seq // tq)
    return pl.pallas_call(
        functools.partial(_attn_kernel, tk=tk, n_kv=seq // tk),
        grid=grid,
        in_specs=[
            pl.BlockSpec((1, MLA_HEADS * MLA_QK, tq), lambda b, i: (b, 0, i)),
            pl.BlockSpec((1, MLA_HEADS, seq, MLA_QK), lambda b, i: (b, 0, 0, 0)),
            pl.BlockSpec((1, MLA_WIDTH, seq), lambda b, i: (b, 0, 0)),
        ],
        out_specs=pl.BlockSpec((1, tq, MLA_WIDTH), lambda b, i: (b, i, 0)),
        out_shape=jax.ShapeDtypeStruct((bsz, seq, MLA_WIDTH), F32),
        compiler_params=pltpu.CompilerParams(
            dimension_semantics=("parallel", "parallel"), vmem_limit_bytes=56 * 2**20),
        name="mla_attn",
    )(qt, kc, vt)


def _out_ffn_kernel(x_ref, ya_ref, ob_ref, gm_ref, wo_ref, g2_ref, wg_ref, wu_ref, wd_ref, gf_ref,
                    out_ref):
    y_b = _rms(ob_ref[0], gm_ref[...]).astype(BF16)
    x1 = (x_ref[0] + _dot(ya_ref[0], wo_ref[:HGRN_WIDTH, :]) + _dot(y_b, wo_ref[HGRN_WIDTH:, :]))
    h2 = _rms(x1, g2_ref[...]).astype(BF16)
    gate = _dot(h2, wg_ref[...])
    up = _dot(h2, wu_ref[...])
    a = (gate * _sigmoid(gate) * up).astype(BF16)
    out_ref[0] = _rms(x1 + _dot(a, wd_ref[...]), gf_ref[...])


def _out_ffn_call(x, ya, ob, gm, wo, g2, wg, wu, wd, gf, *, tm):
    bsz, seq, d = x.shape
    grid = (bsz, seq // tm)
    tok = lambda w: pl.BlockSpec((1, tm, w), lambda b, i: (b, i, 0))
    full = lambda a: pl.BlockSpec(a.shape, lambda b, i: (0,) * a.ndim)
    return pl.pallas_call(
        _out_ffn_kernel,
        grid=grid,
        in_specs=[tok(d), tok(HGRN_WIDTH), tok(MLA_WIDTH), full(gm), full(wo), full(g2),
                  full(wg), full(wu), full(wd), full(gf)],
        out_specs=tok(d),
        out_shape=jax.ShapeDtypeStruct((bsz, seq, d), x.dtype),
        compiler_params=pltpu.CompilerParams(
            dimension_semantics=("parallel", "parallel"), vmem_limit_bytes=56 * 2**20),
        name="out_ffn",
    )(x, ya, ob, gm, wo, g2, wg, wu, wd, gf)


def _rope_tables(seq):
    inv = 1.0 / (ROPE_THETA ** (jnp.arange(0, MLA_ROPE, 2, dtype=F32) / MLA_ROPE))
    ang = jnp.arange(seq, dtype=F32)[:, None] * inv[None, :]
    return jnp.cos(ang), jnp.sin(ang)


def kernel(x, norm1_g, w_in, lb_logits, hgrn_norm_g, q_a_norm_g, w_q_b, kv_a_norm_g, w_kv_b,
           mla_norm_g, w_out, norm2_g, w_gate, w_up, w_down, final_norm_g):
    bsz, seq, d = x.shape
    assert w_in.shape[0] == 1 and lb_logits.shape[1] == 2, "single-layer trunk only"
    n_h = 5 * HGRN_WIDTH

    w_in0 = w_in[0]
    wh = w_in0[:, :n_h].astype(BF16)
    w_kr = w_in0[:, n_h + Q_LORA + KV_LORA:]
    half = MLA_ROPE // 2
    w_kr_rot = jnp.concatenate([-w_kr[:, half:], w_kr[:, :half]], axis=1)
    zpad = jnp.zeros((d, LANES - MLA_ROPE), w_in0.dtype)
    wc = jnp.concatenate([w_in0[:, n_h:n_h + Q_LORA + KV_LORA], w_kr, zpad, w_kr_rot, zpad],
                         axis=1).astype(BF16)
    wqt = w_q_b[0].T.astype(BF16)
    w_kv = w_kv_b[0].reshape(KV_LORA, MLA_HEADS, MLA_NOPE + MLA_V)
    wkn = w_kv[:, :, :MLA_NOPE].reshape(KV_LORA, MLA_HEADS * MLA_NOPE).astype(BF16)
    wvt = w_kv[:, :, MLA_NOPE:].reshape(KV_LORA, MLA_WIDTH).T.astype(BF16)
    cos, sin = _rope_tables(seq)
    cos2 = jnp.concatenate([cos, cos], axis=1)
    sin2 = jnp.concatenate([sin, sin], axis=1)
    row = lambda a: a.reshape(1, -1)

    hq, hi, hf, hb, hg, qt, kc, vt = _in_proj_call(
        x, row(norm1_g[0]), wh, wc, row(q_a_norm_g[0]), row(kv_a_norm_g[0]), wqt, wkn, wvt,
        cos2, sin2, cos.T, sin.T, tm=512, scale=float(MLA_QK) ** -0.5)

    y_a = _hgrn_call(hq, hi, hf, hb, hg, lb_logits.reshape(4, HGRN_WIDTH), row(hgrn_norm_g[0]))
    o_b = _attn_call(qt, kc, vt, tq=256, tk=256)

    return _out_ffn_call(
        x, y_a, o_b, row(mla_norm_g[0]), w_out[0].astype(BF16), row(norm2_g[0]),
        w_gate[0].astype(BF16), w_up[0].astype(BF16), w_down[0].astype(BF16), row(final_norm_g),
        tm=256)
```

```python
import functools
import math

import jax
import jax.numpy as jnp
from jax import lax
from jax.experimental import pallas as pl
from jax.experimental.pallas import tpu as pltpu

EPS = 1e-6
ROPE_THETA = 10000.0
HGRN_HEADS = 8
HGRN_DK = 64
HGRN_WIDTH = HGRN_HEADS * HGRN_DK
HGRN_CHUNK = 64
PAIR = 2 * HGRN_DK
MLA_HEADS = 4
MLA_NOPE = 128
MLA_ROPE = 64
MLA_V = 128
MLA_QK = MLA_NOPE + MLA_ROPE
MLA_WIDTH = MLA_HEADS * MLA_V
Q_LORA = 384
KV_LORA = 256
LANES = 128

BF16 = jnp.bfloat16
F32 = jnp.float32

_NT = (((1,), (1,)), ((), ()))
_TN = (((0,), (0,)), ((), ()))


def _dot(a, b):
    return jnp.dot(a, b, preferred_element_type=F32)


def _dot_nt(a, b):
    return lax.dot_general(a, b, _NT, preferred_element_type=F32)


def _dot_tn(a, b):
    return lax.dot_general(a, b, _TN, preferred_element_type=F32)


def _rms(x, g):
    return x * lax.rsqrt(jnp.mean(x * x, axis=-1, keepdims=True) + EPS) * g


def _sigmoid(x):
    return 1.0 / (1.0 + jnp.exp(-x))


def _split_bf16(x):
    hi = x.astype(BF16)
    lo = (x - hi.astype(F32)).astype(BF16)
    return hi, lo


def _in_proj_kernel(x_ref, g1_ref, wh_ref, wc_ref, gq_ref, gkv_ref, wqt_ref, wkn_ref, wvt_ref,
                    cos2_ref, sin2_ref, cost_ref, sint_ref,
                    hq_ref, hi_ref, hf_ref, hb_ref, hg_ref, qt_ref, kc_ref, vt_ref, *, scale):
    hn = _rms(x_ref[0], g1_ref[...]).astype(BF16)
    for g, o_ref in enumerate((hq_ref, hi_ref, hf_ref, hb_ref, hg_ref)):
        w = wh_ref[:, g * HGRN_WIDTH:(g + 1) * HGRN_WIDTH]
        o_ref[0] = _dot(hn, w).astype(o_ref.dtype)

    c = _dot(hn, wc_ref[...])
    c_q = c[:, :Q_LORA]
    c_kv = c[:, Q_LORA:Q_LORA + KV_LORA]
    k_r = c[:, 640:640 + MLA_ROPE]
    k_rot = c[:, 768:768 + MLA_ROPE]
    cqn = _rms(c_q, gq_ref[...]).astype(BF16)
    ckvn = _rms(c_kv, gkv_ref[...]).astype(BF16)

    qt = _dot_nt(wqt_ref[...], cqn)
    cos_t = cost_ref[...]
    sin_t = sint_ref[...]
    half = MLA_ROPE // 2
    for h in range(MLA_HEADS):
        r0 = h * MLA_QK
        x1 = qt[r0 + MLA_NOPE:r0 + MLA_NOPE + half]
        x2 = qt[r0 + MLA_NOPE + half:r0 + MLA_QK]
        qt_ref[0, r0:r0 + MLA_NOPE, :] = (qt[r0:r0 + MLA_NOPE] * scale).astype(BF16)
        qt_ref[0, r0 + MLA_NOPE:r0 + MLA_NOPE + half, :] = (
            (x1 * cos_t - x2 * sin_t) * scale).astype(BF16)
        qt_ref[0, r0 + MLA_NOPE + half:r0 + MLA_QK, :] = (
            (x1 * sin_t + x2 * cos_t) * scale).astype(BF16)

    k_rope = (k_r * cos2_ref[...] + k_rot * sin2_ref[...]).astype(BF16)
    k_nope = _dot(ckvn, wkn_ref[...])
    for h in range(MLA_HEADS):
        kc_ref[0, h, :, :MLA_NOPE] = k_nope[:, h * MLA_NOPE:(h + 1) * MLA_NOPE].astype(BF16)
        kc_ref[0, h, :, MLA_NOPE:] = k_rope
    vt_ref[0] = _dot_nt(wvt_ref[...], ckvn).astype(BF16)


def _in_proj_call(x, g1, wh, wc, gq, gkv, wqt, wkn, wvt, cos2, sin2, cos_t, sin_t, *, tm, scale):
    bsz, seq, d = x.shape
    grid = (bsz, seq // tm)
    full = lambda a: pl.BlockSpec(a.shape, lambda b, i: (0,) * a.ndim)
    tok = lambda w: pl.BlockSpec((1, tm, w), lambda b, i: (b, i, 0))
    in_specs = [
        tok(d), full(g1), full(wh), full(wc), full(gq), full(gkv), full(wqt), full(wkn), full(wvt),
        pl.BlockSpec((tm, MLA_ROPE), lambda b, i: (i, 0)),
        pl.BlockSpec((tm, MLA_ROPE), lambda b, i: (i, 0)),
        pl.BlockSpec((MLA_ROPE // 2, tm), lambda b, i: (0, i)),
        pl.BlockSpec((MLA_ROPE // 2, tm), lambda b, i: (0, i)),
    ]
    hshape = jax.ShapeDtypeStruct((bsz, seq, HGRN_WIDTH), F32)
    out_shape = [hshape] * 5 + [
        jax.ShapeDtypeStruct((bsz, MLA_HEADS * MLA_QK, seq), BF16),
        jax.ShapeDtypeStruct((bsz, MLA_HEADS, seq, MLA_QK), BF16),
        jax.ShapeDtypeStruct((bsz, MLA_WIDTH, seq), BF16),
    ]
    out_specs = [tok(HGRN_WIDTH)] * 5 + [
        pl.BlockSpec((1, MLA_HEADS * MLA_QK, tm), lambda b, i: (b, 0, i)),
        pl.BlockSpec((1, MLA_HEADS, tm, MLA_QK), lambda b, i: (b, 0, i, 0)),
        pl.BlockSpec((1, MLA_WIDTH, tm), lambda b, i: (b, 0, i)),
    ]
    return pl.pallas_call(
        functools.partial(_in_proj_kernel, scale=scale),
        grid=grid, in_specs=in_specs, out_specs=out_specs, out_shape=out_shape,
        compiler_params=pltpu.CompilerParams(
            dimension_semantics=("parallel", "parallel"), vmem_limit_bytes=56 * 2**20),
        name="in_proj",
    )(x, g1, wh, wc, gq, gkv, wqt, wkn, wvt, cos2, sin2, cos_t, sin_t)


def _hgrn_kernel(hq_ref, hi_ref, hf_ref, hb_ref, hg_ref, lbl_ref, gn_ref, y_ref, oacc_ref, *,
                 n_groups, group):
    c = HGRN_CHUNK
    rg = group * c
    row = lax.broadcasted_iota(jnp.int32, (rg, rg), 0)
    col = lax.broadcasted_iota(jnp.int32, (rg, rg), 1)
    same_chunk = (row // c) == (col // c)
    lane = lax.broadcasted_iota(jnp.int32, (c, PAIR), 1)
    srow = lax.broadcasted_iota(jnp.int32, (c, PAIR), 0)
    head0 = lane < HGRN_DK
    key_pos = jnp.where(head0, lane, lane - HGRN_DK)
    brow = lax.broadcasted_iota(jnp.int32, (PAIR, PAIR), 0)
    bcol = lax.broadcasted_iota(jnp.int32, (PAIR, PAIR), 1)
    same_head = (brow < HGRN_DK) == (bcol < HGRN_DK)
    head_mean = jnp.where(same_head, 1.0 / HGRN_DK, 0.0).astype(BF16)

    lbl = lbl_ref[...]

    def lower_bound(d):
        l0 = lbl[2 * d:2 * d + 1]
        l1 = lbl[2 * d + 1:2 * d + 2]
        m = jnp.maximum(l0, l1)
        e0 = jnp.exp(l0 - m)
        e1 = jnp.exp(l1 - m)
        return e0 / (e0 + e1)

    def sweep_group(n, st, gate_ref, lb, reverse):
        rows = pl.ds(pl.multiple_of(n * rg, rg), rg)
        z = gate_ref[0, rows, :]
        q = hq_ref[0, rows, :]
        q = q * _sigmoid(q)
        v = hi_ref[0, rows, :]
        sg = _sigmoid(z)
        log_f = jnp.log(lb + (1.0 - lb) * sg)
        k = (1.0 - lb) * (1.0 - sg)
        if reverse:
            tri = (same_chunk & (col >= row)).astype(BF16)
            keep = key_pos >= srow
        else:
            tri = (same_chunk & (col <= row)).astype(BF16)
            keep = key_pos <= srow
        lf_hi, lf_lo = _split_bf16(log_f)
        cum = _dot(tri, lf_hi) + _dot(tri, lf_lo)
        q_dec = (q * jnp.exp(cum)).astype(BF16)
        k_inv = k * jnp.exp(-cum)
        v16 = v.astype(BF16)
        zero = jnp.zeros((c, PAIR), F32)
        outs = [None] * group
        for g in (range(group - 1, -1, -1) if reverse else range(group)):
            sl = slice(g * c, (g + 1) * c)
            last = g * c if reverse else (g + 1) * c - 1
            decay = jnp.exp(cum[last:last + 1])
            ki = k_inv[sl]
            k_end = (ki * decay).astype(BF16)
            vg = v[sl]
            qd = q_dec[sl]
            kk = jnp.concatenate([jnp.where(head0, ki, zero), jnp.where(head0, zero, ki)],
                                 axis=0).astype(BF16)
            vv = jnp.concatenate([jnp.where(head0, vg, zero), jnp.where(head0, zero, vg)],
                                 axis=0).astype(BF16)
            scores = _dot_nt(qd, kk)
            p = jnp.where(keep, scores, 0.0).astype(BF16)
            outs[g] = _dot(p, vv) + _dot_nt(qd, st.astype(BF16))
            u_t = _dot_tn(v16[sl], k_end)
            st = st * decay + jnp.where(same_head, u_t, 0.0)
        return rows, jnp.concatenate(outs, axis=0), st

    st0 = jnp.zeros((PAIR, PAIR), F32)

    lb_f = lower_bound(0)

    def fwd_body(n, st):
        rows, o, st = sweep_group(n, st, hf_ref, lb_f, False)
        oacc_ref[rows, :] = o
        return st

    lax.fori_loop(0, n_groups, fwd_body, st0)

    lb_b = lower_bound(1)
    gn = gn_ref[...]

    def bwd_body(i, st):
        rows, o, st = sweep_group(n_groups - 1 - i, st, hb_ref, lb_b, True)
        o = o + oacc_ref[rows, :]
        o2_hi, o2_lo = _split_bf16(o * o)
        ms = _dot(o2_hi, head_mean) + _dot(o2_lo, head_mean)
        g = hg_ref[0, rows, :]
        y_ref[0, rows, :] = (o * lax.rsqrt(ms + EPS) * gn * (g * _sigmoid(g))).astype(y_ref.dtype)
        return st

    lax.fori_loop(0, n_groups, bwd_body, st0)


def _hgrn_call(hq, hi, hf, hb, hg, lbl, gn, *, group):
    bsz, seq, _ = hq.shape
    grid = (bsz, HGRN_WIDTH // PAIR)
    blk = pl.BlockSpec((1, seq, PAIR), lambda b, p: (b, 0, p))
    return pl.pallas_call(
        functools.partial(_hgrn_kernel, n_groups=seq // (HGRN_CHUNK * group), group=group),
        grid=grid,
        in_specs=[blk, blk, blk, blk, blk,
                  pl.BlockSpec((4, PAIR), lambda b, p: (0, p)),
                  pl.BlockSpec((1, PAIR), lambda b, p: (0, p))],
        out_specs=blk,
        out_shape=jax.ShapeDtypeStruct((bsz, seq, HGRN_WIDTH), BF16),
        scratch_shapes=[pltpu.VMEM((seq, PAIR), F32)],
        compiler_params=pltpu.CompilerParams(
            dimension_semantics=("parallel", "parallel"), vmem_limit_bytes=56 * 2**20),
        name="hgrn2",
    )(hq, hi, hf, hb, hg, lbl, gn)


def _attn_kernel(qt_ref, kc_ref, vt_ref, o_ref, *, tk, n_kv):
    tq = qt_ref.shape[2]

    def body(j, carry):
        ks = pl.ds(pl.multiple_of(j * tk, tk), tk)
        out = []
        for h in range(MLA_HEADS):
            m, l, acc = carry[h]
            q_t = qt_ref[0, h * MLA_QK:(h + 1) * MLA_QK, :]
            s = _dot(kc_ref[0, h, ks, :], q_t)
            m_new = jnp.maximum(m, jnp.max(s, axis=0, keepdims=True))
            alpha = jnp.exp2(m - m_new)
            p = jnp.exp2(s - m_new)
            l = alpha * l + jnp.sum(p, axis=0, keepdims=True)
            acc = alpha * acc + _dot(vt_ref[0, h * MLA_V:(h + 1) * MLA_V, ks], p.astype(BF16))
            out.append((m_new, l, acc))
        return tuple(out)

    init = tuple((jnp.full((1, tq), -jnp.inf, F32), jnp.zeros((1, tq), F32),
                  jnp.zeros((MLA_V, tq), F32)) for _ in range(MLA_HEADS))
    res = lax.fori_loop(0, n_kv, body, init)
    for h in range(MLA_HEADS):
        _, l, acc = res[h]
        o_ref[0, :, h * MLA_V:(h + 1) * MLA_V] = (acc / l).T


def _attn_call(qt, kc, vt, *, tq, tk):
    bsz, _, seq = qt.shape
    grid = (bsz, seq // tq)
    return pl.pallas_call(
        functools.partial(_attn_kernel, tk=tk, n_kv=seq // tk),
        grid=grid,
        in_specs=[
            pl.BlockSpec((1, MLA_HEADS * MLA_QK, tq), lambda b, i: (b, 0, i)),
            pl.BlockSpec((1, MLA_HEADS, seq, MLA_QK), lambda b, i: (b, 0, 0, 0)),
            pl.BlockSpec((1, MLA_WIDTH, seq), lambda b, i: (b, 0, 0)),
        ],
        out_specs=pl.BlockSpec((1, tq, MLA_WIDTH), lambda b, i: (b, i, 0)),
        out_shape=jax.ShapeDtypeStruct((bsz, seq, MLA_WIDTH), F32),
        compiler_params=pltpu.CompilerParams(
            dimension_semantics=("parallel", "parallel"), vmem_limit_bytes=56 * 2**20),
        name="mla_attn",
    )(qt, kc, vt)


def _out_ffn_kernel(x_ref, ya_ref, ob_ref, gm_ref, wo_ref, g2_ref, wg_ref, wu_ref, wd_ref, gf_ref,
                    out_ref):
    y_b = _rms(ob_ref[0], gm_ref[...]).astype(BF16)
    x1 = (x_ref[0] + _dot(ya_ref[0], wo_ref[:HGRN_WIDTH, :]) + _dot(y_b, wo_ref[HGRN_WIDTH:, :]))
    h2 = _rms(x1, g2_ref[...]).astype(BF16)
    gate = _dot(h2, wg_ref[...])
    up = _dot(h2, wu_ref[...])
    a = (gate * _sigmoid(gate) * up).astype(BF16)
    out_ref[0] = _rms(x1 + _dot(a, wd_ref[...]), gf_ref[...])


def _out_ffn_call(x, ya, ob, gm, wo, g2, wg, wu, wd, gf, *, tm):
    bsz, seq, d = x.shape
    grid = (bsz, seq // tm)
    tok = lambda w: pl.BlockSpec((1, tm, w), lambda b, i: (b, i, 0))
    full = lambda a: pl.BlockSpec(a.shape, lambda b, i: (0,) * a.ndim)
    return pl.pallas_call(
        _out_ffn_kernel,
        grid=grid,
        in_specs=[tok(d), tok(HGRN_WIDTH), tok(MLA_WIDTH), full(gm), full(wo), full(g2),
                  full(wg), full(wu), full(wd), full(gf)],
        out_specs=tok(d),
        out_shape=jax.ShapeDtypeStruct((bsz, seq, d), x.dtype),
        compiler_params=pltpu.CompilerParams(
            dimension_semantics=("parallel", "parallel"), vmem_limit_bytes=56 * 2**20),
        name="out_ffn",
    )(x, ya, ob, gm, wo, g2, wg, wu, wd, gf)


def _rope_tables(seq):
    inv = 1.0 / (ROPE_THETA ** (jnp.arange(0, MLA_ROPE, 2, dtype=F32) / MLA_ROPE))
    ang = jnp.arange(seq, dtype=F32)[:, None] * inv[None, :]
    return jnp.cos(ang), jnp.sin(ang)


def kernel(x, norm1_g, w_in, lb_logits, hgrn_norm_g, q_a_norm_g, w_q_b, kv_a_norm_g, w_kv_b,
           mla_norm_g, w_out, norm2_g, w_gate, w_up, w_down, final_norm_g):
    bsz, seq, d = x.shape
    assert w_in.shape[0] == 1 and lb_logits.shape[1] == 2, "single-layer trunk only"
    n_h = 5 * HGRN_WIDTH

    w_in0 = w_in[0]
    wh = w_in0[:, :n_h].astype(BF16)
    w_kr = w_in0[:, n_h + Q_LORA + KV_LORA:]
    half = MLA_ROPE // 2
    w_kr_rot = jnp.concatenate([-w_kr[:, half:], w_kr[:, :half]], axis=1)
    zpad = jnp.zeros((d, LANES - MLA_ROPE), w_in0.dtype)
    wc = jnp.concatenate([w_in0[:, n_h:n_h + Q_LORA + KV_LORA], w_kr, zpad, w_kr_rot, zpad],
                         axis=1).astype(BF16)
    wqt = w_q_b[0].T.astype(BF16)
    w_kv = w_kv_b[0].reshape(KV_LORA, MLA_HEADS, MLA_NOPE + MLA_V)
    wkn = w_kv[:, :, :MLA_NOPE].reshape(KV_LORA, MLA_HEADS * MLA_NOPE).astype(BF16)
    wvt = w_kv[:, :, MLA_NOPE:].reshape(KV_LORA, MLA_WIDTH).T.astype(BF16)
    cos, sin = _rope_tables(seq)
    cos2 = jnp.concatenate([cos, cos], axis=1)
    sin2 = jnp.concatenate([sin, sin], axis=1)
    row = lambda a: a.reshape(1, -1)

    hq, hi, hf, hb, hg, qt, kc, vt = _in_proj_call(
        x, row(norm1_g[0]), wh, wc, row(q_a_norm_g[0]), row(kv_a_norm_g[0]), wqt, wkn, wvt,
        cos2, sin2, cos.T, sin.T, tm=512, scale=float(MLA_QK) ** -0.5 * math.log2(math.e))

    y_a = _hgrn_call(hq, hi, hf, hb, hg, lb_logits.reshape(4, HGRN_WIDTH), row(hgrn_norm_g[0]),
                     group=4)
    o_b = _attn_call(qt, kc, vt, tq=256, tk=512)

    return _out_ffn_call(
        x, y_a, o_b, row(mla_norm_g[0]), w_out[0].astype(BF16), row(norm2_g[0]),
        w_gate[0].astype(BF16), w_up[0].astype(BF16), w_down[0].astype(BF16), row(final_norm_g),
        tm=256)
```

```python
import functools
import math

import jax
import jax.numpy as jnp
from jax import lax
from jax.experimental import pallas as pl
from jax.experimental.pallas import tpu as pltpu

EPS = 1e-6
ROPE_THETA = 10000.0
HGRN_HEADS = 8
HGRN_DK = 64
HGRN_WIDTH = HGRN_HEADS * HGRN_DK
HGRN_CHUNK = 64
PAIR = 2 * HGRN_DK
TRI_ROWS = 256
MLA_HEADS = 4
MLA_NOPE = 128
MLA_ROPE = 64
MLA_V = 128
MLA_QK = MLA_NOPE + MLA_ROPE
MLA_WIDTH = MLA_HEADS * MLA_V
Q_LORA = 384
KV_LORA = 256
LANES = 128

BF16 = jnp.bfloat16
F32 = jnp.float32

_NT = (((1,), (1,)), ((), ()))
_TN = (((0,), (0,)), ((), ()))


def _dot(a, b):
    return jnp.dot(a, b, preferred_element_type=F32)


def _dot_nt(a, b):
    return lax.dot_general(a, b, _NT, preferred_element_type=F32)


def _dot_tn(a, b):
    return lax.dot_general(a, b, _TN, preferred_element_type=F32)


def _rms(x, g):
    return x * lax.rsqrt(jnp.mean(x * x, axis=-1, keepdims=True) + EPS) * g


def _sigmoid(x):
    return 1.0 / (1.0 + jnp.exp(-x))


def _split_bf16(x):
    hi = x.astype(BF16)
    lo = (x - hi.astype(F32)).astype(BF16)
    return hi, lo


def _in_proj_kernel(x_ref, g1_ref, wh_ref, wc_ref, gq_ref, gkv_ref, wqt_ref, wkn_ref, wvt_ref,
                    cos2_ref, sin2_ref, cost_ref, sint_ref,
                    hq_ref, hi_ref, hf_ref, hb_ref, hg_ref, qt_ref, kc_ref, vt_ref, *, scale):
    hn = _rms(x_ref[0], g1_ref[...]).astype(BF16)
    for g, o_ref in enumerate((hq_ref, hi_ref, hf_ref, hb_ref, hg_ref)):
        w = wh_ref[:, g * HGRN_WIDTH:(g + 1) * HGRN_WIDTH]
        o_ref[0] = _dot(hn, w).astype(o_ref.dtype)

    c = _dot(hn, wc_ref[...])
    c_q = c[:, :Q_LORA]
    c_kv = c[:, Q_LORA:Q_LORA + KV_LORA]
    k_r = c[:, 640:640 + MLA_ROPE]
    k_rot = c[:, 768:768 + MLA_ROPE]
    cqn = _rms(c_q, gq_ref[...]).astype(BF16)
    ckvn = _rms(c_kv, gkv_ref[...]).astype(BF16)

    qt = _dot_nt(wqt_ref[...], cqn)
    cos_t = cost_ref[...]
    sin_t = sint_ref[...]
    half = MLA_ROPE // 2
    for h in range(MLA_HEADS):
        r0 = h * MLA_QK
        x1 = qt[r0 + MLA_NOPE:r0 + MLA_NOPE + half]
        x2 = qt[r0 + MLA_NOPE + half:r0 + MLA_QK]
        qt_ref[0, r0:r0 + MLA_NOPE, :] = (qt[r0:r0 + MLA_NOPE] * scale).astype(BF16)
        qt_ref[0, r0 + MLA_NOPE:r0 + MLA_NOPE + half, :] = (
            (x1 * cos_t - x2 * sin_t) * scale).astype(BF16)
        qt_ref[0, r0 + MLA_NOPE + half:r0 + MLA_QK, :] = (
            (x1 * sin_t + x2 * cos_t) * scale).astype(BF16)

    k_rope = (k_r * cos2_ref[...] + k_rot * sin2_ref[...]).astype(BF16)
    k_nope = _dot(ckvn, wkn_ref[...])
    for h in range(MLA_HEADS):
        kc_ref[0, h, :, :MLA_NOPE] = k_nope[:, h * MLA_NOPE:(h + 1) * MLA_NOPE].astype(BF16)
        kc_ref[0, h, :, MLA_NOPE:] = k_rope
    vt_ref[0] = _dot_nt(wvt_ref[...], ckvn).astype(BF16)


def _in_proj_call(x, g1, wh, wc, gq, gkv, wqt, wkn, wvt, cos2, sin2, cos_t, sin_t, *, tm, scale):
    bsz, seq, d = x.shape
    grid = (bsz, seq // tm)
    full = lambda a: pl.BlockSpec(a.shape, lambda b, i: (0,) * a.ndim)
    tok = lambda w: pl.BlockSpec((1, tm, w), lambda b, i: (b, i, 0))
    in_specs = [
        tok(d), full(g1), full(wh), full(wc), full(gq), full(gkv), full(wqt), full(wkn), full(wvt),
        pl.BlockSpec((tm, MLA_ROPE), lambda b, i: (i, 0)),
        pl.BlockSpec((tm, MLA_ROPE), lambda b, i: (i, 0)),
        pl.BlockSpec((MLA_ROPE // 2, tm), lambda b, i: (0, i)),
        pl.BlockSpec((MLA_ROPE // 2, tm), lambda b, i: (0, i)),
    ]
    hshape = jax.ShapeDtypeStruct((bsz, seq, HGRN_WIDTH), F32)
    out_shape = [hshape] * 5 + [
        jax.ShapeDtypeStruct((bsz, MLA_HEADS * MLA_QK, seq), BF16),
        jax.ShapeDtypeStruct((bsz, MLA_HEADS, seq, MLA_QK), BF16),
        jax.ShapeDtypeStruct((bsz, MLA_WIDTH, seq), BF16),
    ]
    out_specs = [tok(HGRN_WIDTH)] * 5 + [
        pl.BlockSpec((1, MLA_HEADS * MLA_QK, tm), lambda b, i: (b, 0, i)),
        pl.BlockSpec((1, MLA_HEADS, tm, MLA_QK), lambda b, i: (b, 0, i, 0)),
        pl.BlockSpec((1, MLA_WIDTH, tm), lambda b, i: (b, 0, i)),
    ]
    return pl.pallas_call(
        functools.partial(_in_proj_kernel, scale=scale),
        grid=grid, in_specs=in_specs, out_specs=out_specs, out_shape=out_shape,
        compiler_params=pltpu.CompilerParams(
            dimension_semantics=("parallel", "parallel"), vmem_limit_bytes=56 * 2**20),
        name="in_proj",
    )(x, g1, wh, wc, gq, gkv, wqt, wkn, wvt, cos2, sin2, cos_t, sin_t)


def _hgrn_kernel(hq_ref, hi_ref, hf_ref, hb_ref, hg_ref, lbl_ref, gn_ref, y_ref, oacc_ref, *,
                 n_groups, group):
    c = HGRN_CHUNK
    rg = group * c
    row = lax.broadcasted_iota(jnp.int32, (TRI_ROWS, TRI_ROWS), 0)
    col = lax.broadcasted_iota(jnp.int32, (TRI_ROWS, TRI_ROWS), 1)
    same_chunk = (row // c) == (col // c)
    tri_fwd = (same_chunk & (col <= row)).astype(BF16)
    tri_bwd = (same_chunk & (col >= row)).astype(BF16)
    lane = lax.broadcasted_iota(jnp.int32, (c, PAIR), 1)
    srow = lax.broadcasted_iota(jnp.int32, (c, PAIR), 0)
    head0 = lane < HGRN_DK
    key_pos = jnp.where(head0, lane, lane - HGRN_DK)
    brow = lax.broadcasted_iota(jnp.int32, (PAIR, PAIR), 0)
    bcol = lax.broadcasted_iota(jnp.int32, (PAIR, PAIR), 1)
    same_head = (brow < HGRN_DK) == (bcol < HGRN_DK)
    head_mean = jnp.where(same_head, 1.0 / HGRN_DK, 0.0).astype(BF16)

    lbl = lbl_ref[...]

    def lower_bound(d):
        l0 = lbl[2 * d:2 * d + 1]
        l1 = lbl[2 * d + 1:2 * d + 2]
        m = jnp.maximum(l0, l1)
        e0 = jnp.exp(l0 - m)
        e1 = jnp.exp(l1 - m)
        return e0 / (e0 + e1)

    def sweep_group(n, st, gate_ref, lb, reverse):
        rows = pl.ds(pl.multiple_of(n * rg, rg), rg)
        hq2 = 0.5 * hq_ref[0, rows, :]
        q = hq2 + hq2 * jnp.tanh(hq2)
        v16 = hi_ref[0, rows, :].astype(BF16)
        b = 0.5 * (1.0 - lb)
        bt = b * jnp.tanh(0.5 * gate_ref[0, rows, :])
        log2_f = jnp.log2((0.5 * (1.0 + lb)) + bt)
        k = b - bt
        tri = tri_bwd if reverse else tri_fwd
        keep = (key_pos >= srow) if reverse else (key_pos <= srow)
        lf_hi, lf_lo = _split_bf16(log2_f)
        cum = jnp.concatenate(
            [_dot(tri, lf_hi[r:r + TRI_ROWS]) + _dot(tri, lf_lo[r:r + TRI_ROWS])
             for r in range(0, rg, TRI_ROWS)], axis=0)
        q_dec = (q * jnp.exp2(cum)).astype(BF16)
        k_inv = k * jnp.exp2(-cum)
        k_inv16 = k_inv.astype(BF16)
        zero16 = jnp.zeros((c, PAIR), BF16)
        order = range(group - 1, -1, -1) if reverse else range(group)
        sl = [slice(g * c, (g + 1) * c) for g in range(group)]
        decay, scores, u_t = {}, {}, {}
        for g in order:
            last = g * c if reverse else (g + 1) * c - 1
            decay[g] = jnp.exp2(cum[last:last + 1])
            k_end = (k_inv[sl[g]] * decay[g]).astype(BF16)
            ki = k_inv16[sl[g]]
            kk = jnp.concatenate([jnp.where(head0, ki, zero16), jnp.where(head0, zero16, ki)], axis=0)
            scores[g] = _dot_nt(q_dec[sl[g]], kk)
            u_t[g] = _dot_tn(v16[sl[g]], k_end)
        sts = {}
        for g in order:
            sts[g] = st.astype(BF16)
            st = st * decay[g] + jnp.where(same_head, u_t[g], 0.0)
        outs = [None] * group
        for g in order:
            vg = v16[sl[g]]
            vv = jnp.concatenate([jnp.where(head0, vg, zero16), jnp.where(head0, zero16, vg)], axis=0)
            p = jnp.where(keep, scores[g], 0.0).astype(BF16)
            outs[g] = _dot(p, vv) + _dot_nt(q_dec[sl[g]], sts[g])
        return rows, jnp.concatenate(outs, axis=0), st

    st0 = jnp.zeros((PAIR, PAIR), F32)

    lb_f = lower_bound(0)

    def fwd_body(n, st):
        rows, o, st = sweep_group(n, st, hf_ref, lb_f, False)
        oacc_ref[rows, :] = o
        return st

    lax.fori_loop(0, n_groups, fwd_body, st0)

    lb_b = lower_bound(1)
    gn = gn_ref[...]

    def bwd_body(i, st):
        rows, o, st = sweep_group(n_groups - 1 - i, st, hb_ref, lb_b, True)
        o = o + oacc_ref[rows, :]
        ms = _dot((o * o).astype(BF16), head_mean)
        g2 = 0.5 * hg_ref[0, rows, :]
        gate = g2 + g2 * jnp.tanh(g2)
        y_ref[0, rows, :] = (o * lax.rsqrt(ms + EPS) * gn * gate).astype(y_ref.dtype)
        return st

    lax.fori_loop(0, n_groups, bwd_body, st0)


def _hgrn_call(hq, hi, hf, hb, hg, lbl, gn, *, group):
    bsz, seq, _ = hq.shape
    grid = (bsz, HGRN_WIDTH // PAIR)
    blk = pl.BlockSpec((1, seq, PAIR), lambda b, p: (b, 0, p))
    return pl.pallas_call(
        functools.partial(_hgrn_kernel, n_groups=seq // (HGRN_CHUNK * group), group=group),
        grid=grid,
        in_specs=[blk, blk, blk, blk, blk,
                  pl.BlockSpec((4, PAIR), lambda b, p: (0, p)),
                  pl.BlockSpec((1, PAIR), lambda b, p: (0, p))],
        out_specs=blk,
        out_shape=jax.ShapeDtypeStruct((bsz, seq, HGRN_WIDTH), BF16),
        scratch_shapes=[pltpu.VMEM((seq, PAIR), F32)],
        compiler_params=pltpu.CompilerParams(
            dimension_semantics=("parallel", "parallel"), vmem_limit_bytes=56 * 2**20),
        name="hgrn2",
    )(hq, hi, hf, hb, hg, lbl, gn)


def _attn_kernel(qt_ref, kc_ref, vt_ref, o_ref, *, hps):
    ss = [_dot(kc_ref[0, h], qt_ref[0, h * MLA_QK:(h + 1) * MLA_QK, :]) for h in range(hps)]
    for h in range(hps):
        s = ss[h]
        m = jnp.max(s, axis=0, keepdims=True)
        p = jnp.exp2(s - m)
        l = jnp.sum(p, axis=0, keepdims=True)
        acc = _dot(vt_ref[0, h * MLA_V:(h + 1) * MLA_V, :], p.astype(BF16))
        o_ref[0, :, h * MLA_V:(h + 1) * MLA_V] = (acc / l).T


def _attn_call(qt, kc, vt, *, tq, hps):
    bsz, _, seq = qt.shape
    grid = (bsz, MLA_HEADS // hps, seq // tq)
    return pl.pallas_call(
        functools.partial(_attn_kernel, hps=hps),
        grid=grid,
        in_specs=[
            pl.BlockSpec((1, hps * MLA_QK, tq), lambda b, h, i: (b, h, i)),
            pl.BlockSpec((1, hps, seq, MLA_QK), lambda b, h, i: (b, h, 0, 0)),
            pl.BlockSpec((1, hps * MLA_V, seq), lambda b, h, i: (b, h, 0)),
        ],
        out_specs=pl.BlockSpec((1, tq, hps * MLA_V), lambda b, h, i: (b, i, h)),
        out_shape=jax.ShapeDtypeStruct((bsz, seq, MLA_WIDTH), F32),
        compiler_params=pltpu.CompilerParams(
            dimension_semantics=("parallel", "parallel", "parallel"),
            vmem_limit_bytes=56 * 2**20),
        name="mla_attn",
    )(qt, kc, vt)


def _out_ffn_kernel(x_ref, ya_ref, ob_ref, gm_ref, wo_ref, g2_ref, wg_ref, wu_ref, wd_ref, gf_ref,
                    out_ref):
    y_b = _rms(ob_ref[0], gm_ref[...]).astype(BF16)
    x1 = (x_ref[0] + _dot(ya_ref[0], wo_ref[:HGRN_WIDTH, :]) + _dot(y_b, wo_ref[HGRN_WIDTH:, :]))
    h2 = _rms(x1, g2_ref[...]).astype(BF16)
    gate = _dot(h2, wg_ref[...])
    up = _dot(h2, wu_ref[...])
    a = (gate * _sigmoid(gate) * up).astype(BF16)
    out_ref[0] = _rms(x1 + _dot(a, wd_ref[...]), gf_ref[...])


def _out_ffn_call(x, ya, ob, gm, wo, g2, wg, wu, wd, gf, *, tm):
    bsz, seq, d = x.shape
    grid = (bsz, seq // tm)
    tok = lambda w: pl.BlockSpec((1, tm, w), lambda b, i: (b, i, 0))
    full = lambda a: pl.BlockSpec(a.shape, lambda b, i: (0,) * a.ndim)
    return pl.pallas_call(
        _out_ffn_kernel,
        grid=grid,
        in_specs=[tok(d), tok(HGRN_WIDTH), tok(MLA_WIDTH), full(gm), full(wo), full(g2),
                  full(wg), full(wu), full(wd), full(gf)],
        out_specs=tok(d),
        out_shape=jax.ShapeDtypeStruct((bsz, seq, d), x.dtype),
        compiler_params=pltpu.CompilerParams(
            dimension_semantics=("parallel", "parallel"), vmem_limit_bytes=56 * 2**20),
        name="out_ffn",
    )(x, ya, ob, gm, wo, g2, wg, wu, wd, gf)


def _rope_tables(seq):
    inv = 1.0 / (ROPE_THETA ** (jnp.arange(0, MLA_ROPE, 2, dtype=F32) / MLA_ROPE))
    ang = jnp.arange(seq, dtype=F32)[:, None] * inv[None, :]
    return jnp.cos(ang), jnp.sin(ang)


def kernel(x, norm1_g, w_in, lb_logits, hgrn_norm_g, q_a_norm_g, w_q_b, kv_a_norm_g, w_kv_b,
           mla_norm_g, w_out, norm2_g, w_gate, w_up, w_down, final_norm_g):
    bsz, seq, d = x.shape
    assert w_in.shape[0] == 1 and lb_logits.shape[1] == 2, "single-layer trunk only"
    n_h = 5 * HGRN_WIDTH

    w_in0 = w_in[0]
    wh = w_in0[:, :n_h].astype(BF16)
    w_kr = w_in0[:, n_h + Q_LORA + KV_LORA:]
    half = MLA_ROPE // 2
    w_kr_rot = jnp.concatenate([-w_kr[:, half:], w_kr[:, :half]], axis=1)
    zpad = jnp.zeros((d, LANES - MLA_ROPE), w_in0.dtype)
    wc = jnp.concatenate([w_in0[:, n_h:n_h + Q_LORA + KV_LORA], w_kr, zpad, w_kr_rot, zpad],
                         axis=1).astype(BF16)
    wqt = w_q_b[0].T.astype(BF16)
    w_kv = w_kv_b[0].reshape(KV_LORA, MLA_HEADS, MLA_NOPE + MLA_V)
    wkn = w_kv[:, :, :MLA_NOPE].reshape(KV_LORA, MLA_HEADS * MLA_NOPE).astype(BF16)
    wvt = w_kv[:, :, MLA_NOPE:].reshape(KV_LORA, MLA_WIDTH).T.astype(BF16)
    cos, sin = _rope_tables(seq)
    cos2 = jnp.concatenate([cos, cos], axis=1)
    sin2 = jnp.concatenate([sin, sin], axis=1)
    row = lambda a: a.reshape(1, -1)

    hq, hi, hf, hb, hg, qt, kc, vt = _in_proj_call(
        x, row(norm1_g[0]), wh, wc, row(q_a_norm_g[0]), row(kv_a_norm_g[0]), wqt, wkn, wvt,
        cos2, sin2, cos.T, sin.T, tm=512, scale=float(MLA_QK) ** -0.5 * math.log2(math.e))

    y_a = _hgrn_call(hq, hi, hf, hb, hg, lb_logits.reshape(4, HGRN_WIDTH), row(hgrn_norm_g[0]),
                     group=min(16, seq // HGRN_CHUNK))
    o_b = _attn_call(qt, kc, vt, tq=512, hps=2)

    return _out_ffn_call(
        x, y_a, o_b, row(mla_norm_g[0]), w_out[0].astype(BF16), row(norm2_g[0]),
        w_gate[0].astype(BF16), w_up[0].astype(BF16), w_down[0].astype(BF16), row(final_norm_g),
        tm=256)
```

```python
import functools
import math

import jax
import jax.numpy as jnp
from jax import lax
from jax.experimental import pallas as pl
from jax.experimental.pallas import tpu as pltpu

EPS = 1e-6
ROPE_THETA = 10000.0
HGRN_HEADS = 8
HGRN_DK = 64
HGRN_WIDTH = HGRN_HEADS * HGRN_DK
HGRN_CHUNK = 64
PAIR = 2 * HGRN_DK
TRI_ROWS = 256
MLA_HEADS = 4
MLA_NOPE = 128
MLA_ROPE = 64
MLA_V = 128
MLA_QK = MLA_NOPE + MLA_ROPE
MLA_WIDTH = MLA_HEADS * MLA_V
Q_LORA = 384
KV_LORA = 256
LANES = 128

BF16 = jnp.bfloat16
F32 = jnp.float32

_NT = (((1,), (1,)), ((), ()))
_TN = (((0,), (0,)), ((), ()))


def _dot(a, b):
    return jnp.dot(a, b, preferred_element_type=F32)


def _dot_nt(a, b):
    return lax.dot_general(a, b, _NT, preferred_element_type=F32)


def _dot_tn(a, b):
    return lax.dot_general(a, b, _TN, preferred_element_type=F32)


def _rms(x, g):
    return x * lax.rsqrt(jnp.mean(x * x, axis=-1, keepdims=True) + EPS) * g


def _sigmoid(x):
    return 1.0 / (1.0 + jnp.exp(-x))


def _split_bf16(x):
    hi = x.astype(BF16)
    lo = (x - hi.astype(F32)).astype(BF16)
    return hi, lo


def _in_proj_kernel(x_ref, g1_ref, wh_ref, wc_ref, gq_ref, gkv_ref, wqt_ref, wkn_ref, wvt_ref,
                    cos2_ref, sin2_ref, cost_ref, sint_ref,
                    hq_ref, hi_ref, hf_ref, hb_ref, hg_ref, qt_ref, kc_ref, vt_ref, *, scale):
    hn = _rms(x_ref[0], g1_ref[...]).astype(BF16)
    for g, o_ref in enumerate((hq_ref, hi_ref, hf_ref, hb_ref, hg_ref)):
        w = wh_ref[:, g * HGRN_WIDTH:(g + 1) * HGRN_WIDTH]
        o_ref[0] = _dot(hn, w).astype(o_ref.dtype)

    c = _dot(hn, wc_ref[...])
    c_q = c[:, :Q_LORA]
    c_kv = c[:, Q_LORA:Q_LORA + KV_LORA]
    k_r = c[:, 640:640 + MLA_ROPE]
    k_rot = c[:, 768:768 + MLA_ROPE]
    cqn = _rms(c_q, gq_ref[...]).astype(BF16)
    ckvn = _rms(c_kv, gkv_ref[...]).astype(BF16)

    qt = _dot_nt(wqt_ref[...], cqn)
    cos_t = cost_ref[...]
    sin_t = sint_ref[...]
    half = MLA_ROPE // 2
    for h in range(MLA_HEADS):
        r0 = h * MLA_QK
        x1 = qt[r0 + MLA_NOPE:r0 + MLA_NOPE + half]
        x2 = qt[r0 + MLA_NOPE + half:r0 + MLA_QK]
        qt_ref[0, r0:r0 + MLA_NOPE, :] = (qt[r0:r0 + MLA_NOPE] * scale).astype(BF16)
        qt_ref[0, r0 + MLA_NOPE:r0 + MLA_NOPE + half, :] = (
            (x1 * cos_t - x2 * sin_t) * scale).astype(BF16)
        qt_ref[0, r0 + MLA_NOPE + half:r0 + MLA_QK, :] = (
            (x1 * sin_t + x2 * cos_t) * scale).astype(BF16)

    k_rope = (k_r * cos2_ref[...] + k_rot * sin2_ref[...]).astype(BF16)
    k_nope = _dot(ckvn, wkn_ref[...])
    for h in range(MLA_HEADS):
        kc_ref[0, h, :, :MLA_NOPE] = k_nope[:, h * MLA_NOPE:(h + 1) * MLA_NOPE].astype(BF16)
        kc_ref[0, h, :, MLA_NOPE:] = k_rope
    vt_ref[0] = _dot_nt(wvt_ref[...], ckvn).astype(BF16)


def _in_proj_call(x, g1, wh, wc, gq, gkv, wqt, wkn, wvt, cos2, sin2, cos_t, sin_t, *, tm, scale):
    bsz, seq, d = x.shape
    grid = (bsz, seq // tm)
    full = lambda a: pl.BlockSpec(a.shape, lambda b, i: (0,) * a.ndim)
    tok = lambda w: pl.BlockSpec((1, tm, w), lambda b, i: (b, i, 0))
    in_specs = [
        tok(d), full(g1), full(wh), full(wc), full(gq), full(gkv), full(wqt), full(wkn), full(wvt),
        pl.BlockSpec((tm, MLA_ROPE), lambda b, i: (i, 0)),
        pl.BlockSpec((tm, MLA_ROPE), lambda b, i: (i, 0)),
        pl.BlockSpec((MLA_ROPE // 2, tm), lambda b, i: (0, i)),
        pl.BlockSpec((MLA_ROPE // 2, tm), lambda b, i: (0, i)),
    ]
    hshape = jax.ShapeDtypeStruct((bsz, seq, HGRN_WIDTH), F32)
    out_shape = [hshape] * 5 + [
        jax.ShapeDtypeStruct((bsz, MLA_HEADS * MLA_QK, seq), BF16),
        jax.ShapeDtypeStruct((bsz, MLA_HEADS, seq, MLA_QK), BF16),
        jax.ShapeDtypeStruct((bsz, MLA_WIDTH, seq), BF16),
    ]
    out_specs = [tok(HGRN_WIDTH)] * 5 + [
        pl.BlockSpec((1, MLA_HEADS * MLA_QK, tm), lambda b, i: (b, 0, i)),
        pl.BlockSpec((1, MLA_HEADS, tm, MLA_QK), lambda b, i: (b, 0, i, 0)),
        pl.BlockSpec((1, MLA_WIDTH, tm), lambda b, i: (b, 0, i)),
    ]
    return pl.pallas_call(
        functools.partial(_in_proj_kernel, scale=scale),
        grid=grid, in_specs=in_specs, out_specs=out_specs, out_shape=out_shape,
        compiler_params=pltpu.CompilerParams(
            dimension_semantics=("parallel", "parallel"), vmem_limit_bytes=56 * 2**20),
        name="in_proj",
    )(x, g1, wh, wc, gq, gkv, wqt, wkn, wvt, cos2, sin2, cos_t, sin_t)


def _hgrn_kernel(hq_ref, hi_ref, hf_ref, hb_ref, hg_ref, lbl_ref, gn_ref, y_ref, oacc_ref, *,
                 n_groups, group):
    c = HGRN_CHUNK
    rg = group * c
    row = lax.broadcasted_iota(jnp.int32, (TRI_ROWS, TRI_ROWS), 0)
    col = lax.broadcasted_iota(jnp.int32, (TRI_ROWS, TRI_ROWS), 1)
    same_chunk = (row // c) == (col // c)
    tri_fwd = (same_chunk & (col <= row)).astype(BF16)
    tri_bwd = (same_chunk & (col >= row)).astype(BF16)
    lane = lax.broadcasted_iota(jnp.int32, (c, PAIR), 1)
    srow = lax.broadcasted_iota(jnp.int32, (c, PAIR), 0)
    head0 = lane < HGRN_DK
    key_pos = jnp.where(head0, lane, lane - HGRN_DK)
    brow = lax.broadcasted_iota(jnp.int32, (PAIR, PAIR), 0)
    bcol = lax.broadcasted_iota(jnp.int32, (PAIR, PAIR), 1)
    same_head = (brow < HGRN_DK) == (bcol < HGRN_DK)
    head_mean = jnp.where(same_head, 1.0 / HGRN_DK, 0.0).astype(BF16)

    lbl = lbl_ref[...]

    def lower_bound(d):
        l0 = lbl[2 * d:2 * d + 1]
        l1 = lbl[2 * d + 1:2 * d + 2]
        m = jnp.maximum(l0, l1)
        e0 = jnp.exp(l0 - m)
        e1 = jnp.exp(l1 - m)
        return e0 / (e0 + e1)

    def sweep_group(n, st, gate_ref, lb, reverse):
        rows = pl.ds(pl.multiple_of(n * rg, rg), rg)
        hq2 = 0.5 * hq_ref[0, rows, :]
        q = hq2 + hq2 * jnp.tanh(hq2)
        v16 = hi_ref[0, rows, :].astype(BF16)
        b = 0.5 * (1.0 - lb)
        bt = b * jnp.tanh(0.5 * gate_ref[0, rows, :])
        log2_f = jnp.log2((0.5 * (1.0 + lb)) + bt)
        k = b - bt
        tri = tri_bwd if reverse else tri_fwd
        keep = (key_pos >= srow) if reverse else (key_pos <= srow)
        lf_hi, lf_lo = _split_bf16(log2_f)
        cum = jnp.concatenate(
            [_dot(tri, lf_hi[r:r + TRI_ROWS]) + _dot(tri, lf_lo[r:r + TRI_ROWS])
             for r in range(0, rg, TRI_ROWS)], axis=0)
        q_dec = (q * jnp.exp2(cum)).astype(BF16)
        k_inv = k * jnp.exp2(-cum)
        k_inv16 = k_inv.astype(BF16)
        zero16 = jnp.zeros((c, PAIR), BF16)
        order = range(group - 1, -1, -1) if reverse else range(group)
        sl = [slice(g * c, (g + 1) * c) for g in range(group)]
        decay, scores, u_t = {}, {}, {}
        for g in order:
            last = g * c if reverse else (g + 1) * c - 1
            decay[g] = jnp.exp2(cum[last:last + 1])
            k_end = (k_inv[sl[g]] * decay[g]).astype(BF16)
            ki = k_inv16[sl[g]]
            kk = jnp.concatenate([jnp.where(head0, ki, zero16), jnp.where(head0, zero16, ki)], axis=0)
            scores[g] = _dot_nt(q_dec[sl[g]], kk)
            u_t[g] = _dot_tn(v16[sl[g]], k_end)
        sts = {}
        for g in order:
            sts[g] = st.astype(BF16)
            st = st * decay[g] + jnp.where(same_head, u_t[g], 0.0)
        outs = [None] * group
        for g in order:
            vg = v16[sl[g]]
            vv = jnp.concatenate([jnp.where(head0, vg, zero16), jnp.where(head0, zero16, vg)], axis=0)
            p = jnp.where(keep, scores[g], 0.0).astype(BF16)
            outs[g] = _dot(p, vv) + _dot_nt(q_dec[sl[g]], sts[g])
        return rows, jnp.concatenate(outs, axis=0), st

    st0 = jnp.zeros((PAIR, PAIR), F32)

    lb_f = lower_bound(0)

    def fwd_body(n, st):
        rows, o, st = sweep_group(n, st, hf_ref, lb_f, False)
        oacc_ref[rows, :] = o
        return st

    lax.fori_loop(0, n_groups, fwd_body, st0)

    lb_b = lower_bound(1)
    gn = gn_ref[...]

    def bwd_body(i, st):
        rows, o, st = sweep_group(n_groups - 1 - i, st, hb_ref, lb_b, True)
        o = o + oacc_ref[rows, :]
        ms = _dot((o * o).astype(BF16), head_mean)
        g2 = 0.5 * hg_ref[0, rows, :]
        gate = g2 + g2 * jnp.tanh(g2)
        y_ref[0, rows, :] = (o * lax.rsqrt(ms + EPS) * gn * gate).astype(y_ref.dtype)
        return st

    lax.fori_loop(0, n_groups, bwd_body, st0)


def _hgrn_call(hq, hi, hf, hb, hg, lbl, gn, *, group):
    bsz, seq, _ = hq.shape
    grid = (bsz, HGRN_WIDTH // PAIR)
    blk = pl.BlockSpec((1, seq, PAIR), lambda b, p: (b, 0, p))
    return pl.pallas_call(
        functools.partial(_hgrn_kernel, n_groups=seq // (HGRN_CHUNK * group), group=group),
        grid=grid,
        in_specs=[blk, blk, blk, blk, blk,
                  pl.BlockSpec((4, PAIR), lambda b, p: (0, p)),
                  pl.BlockSpec((1, PAIR), lambda b, p: (0, p))],
        out_specs=blk,
        out_shape=jax.ShapeDtypeStruct((bsz, seq, HGRN_WIDTH), BF16),
        scratch_shapes=[pltpu.VMEM((seq, PAIR), F32)],
        compiler_params=pltpu.CompilerParams(
            dimension_semantics=("parallel", "parallel"), vmem_limit_bytes=56 * 2**20),
        name="hgrn2",
    )(hq, hi, hf, hb, hg, lbl, gn)


def _attn_kernel(qt_ref, kc_ref, vt_ref, o_ref, s0_ref, s1_ref, m0_ref, m1_ref, acc_ref, *, tk):
    r = pl.program_id(0)
    seq = kc_ref.shape[2]

    @pl.when(r == 0)
    def _():
        s1_ref[...] = jnp.zeros_like(s1_ref)
        m1_ref[...] = jnp.zeros_like(m1_ref)

    def step(s_cur, s_prev, m_cur_ref, m_prev_ref):
        q_t = qt_ref[0]
        m_prev = m_prev_ref[...]
        m_cur = None
        l = None
        for c in range(seq // tk):
            ks = slice(c * tk, (c + 1) * tk)
            s = _dot(kc_ref[0, 0, ks, :], q_t)
            cm = jnp.max(s, axis=0, keepdims=True)
            m_cur = cm if m_cur is None else jnp.maximum(m_cur, cm)
            p = jnp.exp2(s_prev[ks, :] - m_prev)
            s_cur[ks, :] = s
            ps = jnp.sum(p, axis=0, keepdims=True)
            l = ps if l is None else l + ps
            pv = _dot(vt_ref[0, :, ks], p.astype(BF16))
            if c == 0:
                acc_ref[...] = pv
            else:
                acc_ref[...] += pv
        m_cur_ref[...] = m_cur
        o_ref[0] = (acc_ref[...] / l).T

    @pl.when(r % 2 == 0)
    def _():
        step(s0_ref, s1_ref, m0_ref, m1_ref)

    @pl.when(r % 2 == 1)
    def _():
        step(s1_ref, s0_ref, m1_ref, m0_ref)


def _attn_call(qt, kc, vt, *, tq, tk):
    bsz, _, seq = qt.shape
    nq = seq // tq
    n_prob = bsz * MLA_HEADS * nq

    def prob(r):
        r = jnp.minimum(r, n_prob - 1)
        return r // (MLA_HEADS * nq), (r // nq) % MLA_HEADS, r % nq

    def lag(r):
        return prob(jnp.maximum(r - 1, 0))

    return pl.pallas_call(
        functools.partial(_attn_kernel, tk=tk),
        grid=(n_prob + 1,),
        in_specs=[
            pl.BlockSpec((1, MLA_QK, tq), lambda r: prob(r)),
            pl.BlockSpec((1, 1, seq, MLA_QK), lambda r: prob(r)[:2] + (0, 0)),
            pl.BlockSpec((1, MLA_V, seq), lambda r: lag(r)[:2] + (0,)),
        ],
        out_specs=pl.BlockSpec((1, tq, MLA_V), lambda r: (lag(r)[0], lag(r)[2], lag(r)[1])),
        out_shape=jax.ShapeDtypeStruct((bsz, seq, MLA_WIDTH), F32),
        scratch_shapes=[pltpu.VMEM((seq, tq), F32), pltpu.VMEM((seq, tq), F32),
                        pltpu.VMEM((1, tq), F32), pltpu.VMEM((1, tq), F32),
                        pltpu.VMEM((MLA_V, tq), F32)],
        compiler_params=pltpu.CompilerParams(
            dimension_semantics=("arbitrary",), vmem_limit_bytes=56 * 2**20),
        name="mla_attn",
    )(qt, kc, vt)


def _out_ffn_kernel(x_ref, ya_ref, ob_ref, gm_ref, wo_ref, g2_ref, wg_ref, wu_ref, wd_ref, gf_ref,
                    out_ref, *, n_sub):
    tm = x_ref.shape[1]
    ts = tm // n_sub
    subs = [slice(i * ts, (i + 1) * ts) for i in range(n_sub)]
    x1 = [x_ref[0, r, :] + _dot(ya_ref[0, r, :], wo_ref[:HGRN_WIDTH, :])
          + _dot(_rms(ob_ref[0, r, :], gm_ref[...]).astype(BF16), wo_ref[HGRN_WIDTH:, :]) for r in subs]
    h2 = [_rms(v, g2_ref[...]).astype(BF16) for v in x1]
    gate = [_dot(h, wg_ref[...]) for h in h2]
    up = [_dot(h, wu_ref[...]) for h in h2]
    a = [(g * _sigmoid(g) * u).astype(BF16) for g, u in zip(gate, up)]
    for r, v, ai in zip(subs, x1, a):
        out_ref[0, r, :] = _rms(v + _dot(ai, wd_ref[...]), gf_ref[...])


def _out_ffn_call(x, ya, ob, gm, wo, g2, wg, wu, wd, gf, *, tm, n_sub):
    bsz, seq, d = x.shape
    grid = (bsz, seq // tm)
    tok = lambda w: pl.BlockSpec((1, tm, w), lambda b, i: (b, i, 0))
    full = lambda a: pl.BlockSpec(a.shape, lambda b, i: (0,) * a.ndim,
                                  pipeline_mode=pl.Buffered(1))
    return pl.pallas_call(
        functools.partial(_out_ffn_kernel, n_sub=n_sub),
        grid=grid,
        in_specs=[tok(d), tok(HGRN_WIDTH), tok(MLA_WIDTH), full(gm), full(wo), full(g2),
                  full(wg), full(wu), full(wd), full(gf)],
        out_specs=tok(d),
        out_shape=jax.ShapeDtypeStruct((bsz, seq, d), x.dtype),
        compiler_params=pltpu.CompilerParams(
            dimension_semantics=("parallel", "parallel"), vmem_limit_bytes=56 * 2**20),
        name="out_ffn",
    )(x, ya, ob, gm, wo, g2, wg, wu, wd, gf)


def _rope_tables(seq):
    inv = 1.0 / (ROPE_THETA ** (jnp.arange(0, MLA_ROPE, 2, dtype=F32) / MLA_ROPE))
    ang = jnp.arange(seq, dtype=F32)[:, None] * inv[None, :]
    return jnp.cos(ang), jnp.sin(ang)


def kernel(x, norm1_g, w_in, lb_logits, hgrn_norm_g, q_a_norm_g, w_q_b, kv_a_norm_g, w_kv_b,
           mla_norm_g, w_out, norm2_g, w_gate, w_up, w_down, final_norm_g):
    bsz, seq, d = x.shape
    assert w_in.shape[0] == 1 and lb_logits.shape[1] == 2, "single-layer trunk only"
    n_h = 5 * HGRN_WIDTH

    w_in0 = w_in[0]
    wh = w_in0[:, :n_h].astype(BF16)
    w_kr = w_in0[:, n_h + Q_LORA + KV_LORA:]
    half = MLA_ROPE // 2
    w_kr_rot = jnp.concatenate([-w_kr[:, half:], w_kr[:, :half]], axis=1)
    zpad = jnp.zeros((d, LANES - MLA_ROPE), w_in0.dtype)
    wc = jnp.concatenate([w_in0[:, n_h:n_h + Q_LORA + KV_LORA], w_kr, zpad, w_kr_rot, zpad],
                         axis=1).astype(BF16)
    wqt = w_q_b[0].T.astype(BF16)
    w_kv = w_kv_b[0].reshape(KV_LORA, MLA_HEADS, MLA_NOPE + MLA_V)
    wkn = w_kv[:, :, :MLA_NOPE].reshape(KV_LORA, MLA_HEADS * MLA_NOPE).astype(BF16)
    wvt = w_kv[:, :, MLA_NOPE:].reshape(KV_LORA, MLA_WIDTH).T.astype(BF16)
    cos, sin = _rope_tables(seq)
    cos2 = jnp.concatenate([cos, cos], axis=1)
    sin2 = jnp.concatenate([sin, sin], axis=1)
    row = lambda a: a.reshape(1, -1)

    hq, hi, hf, hb, hg, qt, kc, vt = _in_proj_call(
        x, row(norm1_g[0]), wh, wc, row(q_a_norm_g[0]), row(kv_a_norm_g[0]), wqt, wkn, wvt,
        cos2, sin2, cos.T, sin.T, tm=512, scale=float(MLA_QK) ** -0.5 * math.log2(math.e))

    y_a = _hgrn_call(hq, hi, hf, hb, hg, lb_logits.reshape(4, HGRN_WIDTH), row(hgrn_norm_g[0]),
                     group=min(32, seq // HGRN_CHUNK))
    o_b = _attn_call(qt, kc, vt, tq=512, tk=512)

    return _out_ffn_call(
        x, y_a, o_b, row(mla_norm_g[0]), w_out[0].astype(BF16), row(norm2_g[0]),
        w_gate[0].astype(BF16), w_up[0].astype(BF16), w_down[0].astype(BF16), row(final_norm_g),
        tm=512, n_sub=2)
```

```python
import functools
import math
from typing import NamedTuple

import jax
import jax.numpy as jnp
from jax import lax
from jax.experimental import pallas as pl
from jax.experimental.pallas import tpu as pltpu

EPS = 1e-6
ROPE_THETA = 10000.0
HGRN_HEADS = 8
HGRN_DK = 64
HGRN_WIDTH = HGRN_HEADS * HGRN_DK
HGRN_CHUNK = 64
PAIR = 2 * HGRN_DK
TRI_ROWS = 256
MLA_HEADS = 4
MLA_NOPE = 128
MLA_ROPE = 64
MLA_V = 128
MLA_QK = MLA_NOPE + MLA_ROPE
MLA_WIDTH = MLA_HEADS * MLA_V
Q_LORA = 384
KV_LORA = 256
LANES = 128
V7X_VMEM_BYTES = 64 * 2**20
VMEM_LIMIT = V7X_VMEM_BYTES * 7 // 8

BF16 = jnp.bfloat16
F32 = jnp.float32

_NT = (((1,), (1,)), ((), ()))
_TN = (((0,), (0,)), ((), ()))


def _dot(a, b):
    return jnp.dot(a, b, preferred_element_type=F32)


def _dot_nt(a, b):
    return lax.dot_general(a, b, _NT, preferred_element_type=F32)


def _dot_tn(a, b):
    return lax.dot_general(a, b, _TN, preferred_element_type=F32)


def _rms(x, g):
    return x * lax.rsqrt(jnp.mean(x * x, axis=-1, keepdims=True) + EPS) * g


def _sigmoid(x):
    return 1.0 / (1.0 + jnp.exp(-x))


def _split_bf16(x):
    hi = x.astype(BF16)
    lo = (x - hi.astype(F32)).astype(BF16)
    return hi, lo


def _in_proj_kernel(x_ref, g1_ref, wh_ref, wc_ref, gq_ref, gkv_ref, wqt_ref, wkn_ref, wvt_ref,
                    cos2_ref, sin2_ref, cost_ref, sint_ref,
                    hq_ref, hi_ref, hf_ref, hb_ref, hg_ref, qt_ref, kc_ref, vt_ref, *, scale):
    hn = _rms(x_ref[0], g1_ref[...]).astype(BF16)
    for g, o_ref in enumerate((hq_ref, hi_ref, hf_ref, hb_ref, hg_ref)):
        w = wh_ref[:, g * HGRN_WIDTH:(g + 1) * HGRN_WIDTH]
        o_ref[0] = _dot(hn, w).astype(o_ref.dtype)

    c = _dot(hn, wc_ref[...])
    c_q = c[:, :Q_LORA]
    c_kv = c[:, Q_LORA:Q_LORA + KV_LORA]
    kr0 = Q_LORA + KV_LORA
    k_r = c[:, kr0:kr0 + MLA_ROPE]
    k_rot = c[:, kr0 + LANES:kr0 + LANES + MLA_ROPE]
    cqn = _rms(c_q, gq_ref[...]).astype(BF16)
    ckvn = _rms(c_kv, gkv_ref[...]).astype(BF16)

    qt = _dot_nt(wqt_ref[...], cqn)
    cos_t = cost_ref[...]
    sin_t = sint_ref[...]
    half = MLA_ROPE // 2
    for h in range(MLA_HEADS):
        r0 = h * MLA_QK
        x1 = qt[r0 + MLA_NOPE:r0 + MLA_NOPE + half]
        x2 = qt[r0 + MLA_NOPE + half:r0 + MLA_QK]
        qt_ref[0, r0:r0 + MLA_NOPE, :] = (qt[r0:r0 + MLA_NOPE] * scale).astype(BF16)
        qt_ref[0, r0 + MLA_NOPE:r0 + MLA_NOPE + half, :] = (
            (x1 * cos_t - x2 * sin_t) * scale).astype(BF16)
        qt_ref[0, r0 + MLA_NOPE + half:r0 + MLA_QK, :] = (
            (x1 * sin_t + x2 * cos_t) * scale).astype(BF16)

    k_rope = (k_r * cos2_ref[...] + k_rot * sin2_ref[...]).astype(BF16)
    k_nope = _dot(ckvn, wkn_ref[...])
    for h in range(MLA_HEADS):
        kc_ref[0, h, :, :MLA_NOPE] = k_nope[:, h * MLA_NOPE:(h + 1) * MLA_NOPE].astype(BF16)
        kc_ref[0, h, :, MLA_NOPE:] = k_rope
    vt_ref[0] = _dot_nt(wvt_ref[...], ckvn).astype(BF16)


def _in_proj_call(x, g1, wh, wc, gq, gkv, wqt, wkn, wvt, cos2, sin2, cos_t, sin_t, *, tm, scale):
    bsz, seq, d = x.shape
    grid = (bsz, seq // tm)
    full = lambda a: pl.BlockSpec(a.shape, lambda b, i: (0,) * a.ndim)
    tok = lambda w: pl.BlockSpec((1, tm, w), lambda b, i: (b, i, 0))
    in_specs = [
        tok(d), full(g1), full(wh), full(wc), full(gq), full(gkv), full(wqt), full(wkn), full(wvt),
        pl.BlockSpec((tm, MLA_ROPE), lambda b, i: (i, 0)),
        pl.BlockSpec((tm, MLA_ROPE), lambda b, i: (i, 0)),
        pl.BlockSpec((MLA_ROPE // 2, tm), lambda b, i: (0, i)),
        pl.BlockSpec((MLA_ROPE // 2, tm), lambda b, i: (0, i)),
    ]
    hshape = jax.ShapeDtypeStruct((bsz, seq, HGRN_WIDTH), F32)
    out_shape = [hshape] * 5 + [
        jax.ShapeDtypeStruct((bsz, MLA_HEADS * MLA_QK, seq), BF16),
        jax.ShapeDtypeStruct((bsz, MLA_HEADS, seq, MLA_QK), BF16),
        jax.ShapeDtypeStruct((bsz, MLA_WIDTH, seq), BF16),
    ]
    out_specs = [tok(HGRN_WIDTH)] * 5 + [
        pl.BlockSpec((1, MLA_HEADS * MLA_QK, tm), lambda b, i: (b, 0, i)),
        pl.BlockSpec((1, MLA_HEADS, tm, MLA_QK), lambda b, i: (b, 0, i, 0)),
        pl.BlockSpec((1, MLA_WIDTH, tm), lambda b, i: (b, 0, i)),
    ]
    return pl.pallas_call(
        functools.partial(_in_proj_kernel, scale=scale),
        grid=grid, in_specs=in_specs, out_specs=out_specs, out_shape=out_shape,
        compiler_params=pltpu.CompilerParams(
            dimension_semantics=("parallel", "parallel"), vmem_limit_bytes=VMEM_LIMIT),
        name="in_proj",
    )(x, g1, wh, wc, gq, gkv, wqt, wkn, wvt, cos2, sin2, cos_t, sin_t)


def _hgrn_kernel(hq_ref, hi_ref, hf_ref, hb_ref, hg_ref, lbl_ref, gn_ref, y_ref, oacc_ref, *,
                 n_groups, group):
    c = HGRN_CHUNK
    rg = group * c
    row = lax.broadcasted_iota(jnp.int32, (TRI_ROWS, TRI_ROWS), 0)
    col = lax.broadcasted_iota(jnp.int32, (TRI_ROWS, TRI_ROWS), 1)
    same_chunk = (row // c) == (col // c)
    tri_fwd = (same_chunk & (col <= row)).astype(BF16)
    tri_bwd = (same_chunk & (col >= row)).astype(BF16)
    lane = lax.broadcasted_iota(jnp.int32, (c, PAIR), 1)
    srow = lax.broadcasted_iota(jnp.int32, (c, PAIR), 0)
    head0 = lane < HGRN_DK
    key_pos = jnp.where(head0, lane, lane - HGRN_DK)
    brow = lax.broadcasted_iota(jnp.int32, (PAIR, PAIR), 0)
    bcol = lax.broadcasted_iota(jnp.int32, (PAIR, PAIR), 1)
    same_head = (brow < HGRN_DK) == (bcol < HGRN_DK)
    head_mean = jnp.where(same_head, 1.0 / HGRN_DK, 0.0).astype(BF16)

    lbl = lbl_ref[...]

    def lower_bound(d):
        l0 = lbl[2 * d:2 * d + 1]
        l1 = lbl[2 * d + 1:2 * d + 2]
        m = jnp.maximum(l0, l1)
        e0 = jnp.exp(l0 - m)
        e1 = jnp.exp(l1 - m)
        return e0 / (e0 + e1)

    def sweep_group(n, st, gate_ref, lb, reverse):
        rows = pl.ds(pl.multiple_of(n * rg, rg), rg)
        hq2 = 0.5 * hq_ref[0, rows, :]
        q = hq2 + hq2 * jnp.tanh(hq2)
        v16 = hi_ref[0, rows, :].astype(BF16)
        b = 0.5 * (1.0 - lb)
        bt = b * jnp.tanh(0.5 * gate_ref[0, rows, :])
        log2_f = jnp.log2((0.5 * (1.0 + lb)) + bt)
        k = b - bt
        tri = tri_bwd if reverse else tri_fwd
        keep = (key_pos >= srow) if reverse else (key_pos <= srow)
        lf_hi, lf_lo = _split_bf16(log2_f)
        cum = jnp.concatenate(
            [_dot(tri, lf_hi[r:r + TRI_ROWS]) + _dot(tri, lf_lo[r:r + TRI_ROWS])
             for r in range(0, rg, TRI_ROWS)], axis=0)
        q_dec = (q * jnp.exp2(cum)).astype(BF16)
        k_inv = k * jnp.exp2(-cum)
        k_inv16 = k_inv.astype(BF16)
        zero16 = jnp.zeros((c, PAIR), BF16)
        order = range(group - 1, -1, -1) if reverse else range(group)
        sl = [slice(g * c, (g + 1) * c) for g in range(group)]
        decay, scores, u_t = {}, {}, {}
        for g in order:
            last = g * c if reverse else (g + 1) * c - 1
            decay[g] = jnp.exp2(cum[last:last + 1])
            k_end = (k_inv[sl[g]] * decay[g]).astype(BF16)
            ki = k_inv16[sl[g]]
            kk = jnp.concatenate([jnp.where(head0, ki, zero16), jnp.where(head0, zero16, ki)], axis=0)
            scores[g] = _dot_nt(q_dec[sl[g]], kk)
            u_t[g] = _dot_tn(v16[sl[g]], k_end)
        sts = {}
        for g in order:
            sts[g] = st.T.astype(BF16)
            st = st * decay[g] + jnp.where(same_head, u_t[g], 0.0)
        outs = [None] * group
        for g in order:
            vg = v16[sl[g]]
            vv = jnp.concatenate([jnp.where(head0, vg, zero16), jnp.where(head0, zero16, vg)], axis=0)
            p = jnp.where(keep, scores[g], 0.0).astype(BF16)
            outs[g] = _dot(p, vv) + _dot(q_dec[sl[g]], sts[g])
        return rows, jnp.concatenate(outs, axis=0), st

    st0 = jnp.zeros((PAIR, PAIR), F32)

    lb_f = lower_bound(0)

    def fwd_body(n, st):
        rows, o, st = sweep_group(n, st, hf_ref, lb_f, False)
        oacc_ref[rows, :] = o
        return st

    lax.fori_loop(0, n_groups, fwd_body, st0)

    lb_b = lower_bound(1)
    gn = gn_ref[...]

    def bwd_body(i, st):
        rows, o, st = sweep_group(n_groups - 1 - i, st, hb_ref, lb_b, True)
        o = o + oacc_ref[rows, :]
        ms = _dot((o * o).astype(BF16), head_mean)
        g2 = 0.5 * hg_ref[0, rows, :]
        gate = g2 + g2 * jnp.tanh(g2)
        y_ref[0, rows, :] = (o * lax.rsqrt(ms + EPS) * gn * gate).astype(y_ref.dtype)
        return st

    lax.fori_loop(0, n_groups, bwd_body, st0)


def _hgrn_call(hq, hi, hf, hb, hg, lbl, gn, *, group):
    bsz, seq, _ = hq.shape
    grid = (bsz, HGRN_WIDTH // PAIR)
    blk = pl.BlockSpec((1, seq, PAIR), lambda b, p: (b, 0, p))
    return pl.pallas_call(
        functools.partial(_hgrn_kernel, n_groups=seq // (HGRN_CHUNK * group), group=group),
        grid=grid,
        in_specs=[blk, blk, blk, blk, blk,
                  pl.BlockSpec((4, PAIR), lambda b, p: (0, p)),
                  pl.BlockSpec((1, PAIR), lambda b, p: (0, p))],
        out_specs=blk,
        out_shape=jax.ShapeDtypeStruct((bsz, seq, HGRN_WIDTH), BF16),
        scratch_shapes=[pltpu.VMEM((seq, PAIR), F32)],
        compiler_params=pltpu.CompilerParams(
            dimension_semantics=("parallel", "parallel"), vmem_limit_bytes=VMEM_LIMIT),
        name="hgrn2",
    )(hq, hi, hf, hb, hg, lbl, gn)


def _attn_kernel(qt_ref, kc_ref, vt_ref, o_ref, s0_ref, s1_ref, m0_ref, m1_ref, acc_ref, *, tk):
    r = pl.program_id(0)
    seq = kc_ref.shape[2]

    @pl.when(r == 0)
    def _():
        s1_ref[...] = jnp.zeros_like(s1_ref)
        m1_ref[...] = jnp.zeros_like(m1_ref)

    def step(s_cur, s_prev, m_cur_ref, m_prev_ref):
        q_t = qt_ref[0]
        m_prev = m_prev_ref[...]
        m_cur = None
        l = None
        for c in range(seq // tk):
            ks = slice(c * tk, (c + 1) * tk)
            s = _dot(kc_ref[0, 0, ks, :], q_t)
            cm = jnp.max(s, axis=0, keepdims=True)
            m_cur = cm if m_cur is None else jnp.maximum(m_cur, cm)
            p = jnp.exp2(s_prev[ks, :] - m_prev)
            s_cur[ks, :] = s
            ps = jnp.sum(p, axis=0, keepdims=True)
            l = ps if l is None else l + ps
            pv = _dot(vt_ref[0, :, ks], p.astype(BF16))
            if c == 0:
                acc_ref[...] = pv
            else:
                acc_ref[...] += pv
        m_cur_ref[...] = m_cur
        o_ref[0] = (acc_ref[...] / l).T

    @pl.when(r % 2 == 0)
    def _():
        step(s0_ref, s1_ref, m0_ref, m1_ref)

    @pl.when(r % 2 == 1)
    def _():
        step(s1_ref, s0_ref, m1_ref, m0_ref)


def _attn_call(qt, kc, vt, *, tq, tk):
    bsz, _, seq = qt.shape
    nq = seq // tq
    n_prob = bsz * MLA_HEADS * nq

    def prob(r):
        r = jnp.minimum(r, n_prob - 1)
        return r // (MLA_HEADS * nq), (r // nq) % MLA_HEADS, r % nq

    def lag(r):
        return prob(jnp.maximum(r - 1, 0))

    return pl.pallas_call(
        functools.partial(_attn_kernel, tk=tk),
        grid=(n_prob + 1,),
        in_specs=[
            pl.BlockSpec((1, MLA_QK, tq), lambda r: prob(r)),
            pl.BlockSpec((1, 1, seq, MLA_QK), lambda r: prob(r)[:2] + (0, 0)),
            pl.BlockSpec((1, MLA_V, seq), lambda r: lag(r)[:2] + (0,)),
        ],
        out_specs=pl.BlockSpec((1, tq, MLA_V), lambda r: (lag(r)[0], lag(r)[2], lag(r)[1])),
        out_shape=jax.ShapeDtypeStruct((bsz, seq, MLA_WIDTH), F32),
        scratch_shapes=[pltpu.VMEM((seq, tq), F32), pltpu.VMEM((seq, tq), F32),
                        pltpu.VMEM((1, tq), F32), pltpu.VMEM((1, tq), F32),
                        pltpu.VMEM((MLA_V, tq), F32)],
        compiler_params=pltpu.CompilerParams(
            dimension_semantics=("arbitrary",), vmem_limit_bytes=VMEM_LIMIT),
        name="mla_attn",
    )(qt, kc, vt)


def _out_ffn_kernel(x_ref, ya_ref, ob_ref, gm_ref, wo_ref, g2_ref, wg_ref, wu_ref, wd_ref, gf_ref,
                    out_ref, *, n_sub):
    tm = x_ref.shape[1]
    ts = tm // n_sub
    subs = [slice(i * ts, (i + 1) * ts) for i in range(n_sub)]
    x1 = [x_ref[0, r, :] + _dot(ya_ref[0, r, :], wo_ref[:HGRN_WIDTH, :])
          + _dot(_rms(ob_ref[0, r, :], gm_ref[...]).astype(BF16), wo_ref[HGRN_WIDTH:, :]) for r in subs]
    h2 = [_rms(v, g2_ref[...]).astype(BF16) for v in x1]
    gate = [_dot(h, wg_ref[...]) for h in h2]
    up = [_dot(h, wu_ref[...]) for h in h2]
    a = [(g * _sigmoid(g) * u).astype(BF16) for g, u in zip(gate, up)]
    for r, v, ai in zip(subs, x1, a):
        out_ref[0, r, :] = _rms(v + _dot(ai, wd_ref[...]), gf_ref[...])


def _out_ffn_call(x, ya, ob, gm, wo, g2, wg, wu, wd, gf, *, tm, n_sub):
    bsz, seq, d = x.shape
    grid = (bsz, seq // tm)
    tok = lambda w: pl.BlockSpec((1, tm, w), lambda b, i: (b, i, 0))
    full = lambda a: pl.BlockSpec(a.shape, lambda b, i: (0,) * a.ndim,
                                  pipeline_mode=pl.Buffered(1))
    return pl.pallas_call(
        functools.partial(_out_ffn_kernel, n_sub=n_sub),
        grid=grid,
        in_specs=[tok(d), tok(HGRN_WIDTH), tok(MLA_WIDTH), full(gm), full(wo), full(g2),
                  full(wg), full(wu), full(wd), full(gf)],
        out_specs=tok(d),
        out_shape=jax.ShapeDtypeStruct((bsz, seq, d), x.dtype),
        compiler_params=pltpu.CompilerParams(
            dimension_semantics=("parallel", "parallel"), vmem_limit_bytes=VMEM_LIMIT),
        name="out_ffn",
    )(x, ya, ob, gm, wo, g2, wg, wu, wd, gf)


class _Tiles(NamedTuple):
    proj_rows: int
    hgrn_group: int
    attn_q: int
    attn_k: int
    ffn_rows: int
    ffn_sub: int


def _tiles(seq):
    rows = min(512, seq)
    return _Tiles(proj_rows=rows, hgrn_group=min(32, seq // HGRN_CHUNK), attn_q=rows, attn_k=rows,
                  ffn_rows=rows, ffn_sub=2)


def _rope_tables(seq):
    inv = 1.0 / (ROPE_THETA ** (jnp.arange(0, MLA_ROPE, 2, dtype=F32) / MLA_ROPE))
    ang = jnp.arange(seq, dtype=F32)[:, None] * inv[None, :]
    return jnp.cos(ang), jnp.sin(ang)


def kernel(x, norm1_g, w_in, lb_logits, hgrn_norm_g, q_a_norm_g, w_q_b, kv_a_norm_g, w_kv_b,
           mla_norm_g, w_out, norm2_g, w_gate, w_up, w_down, final_norm_g):
    bsz, seq, d = x.shape
    assert w_in.shape[0] == 1 and lb_logits.shape[1] == 2, "single-layer trunk only"
    t = _tiles(seq)
    n_h = 5 * HGRN_WIDTH

    w_in0 = w_in[0]
    wh = w_in0[:, :n_h].astype(BF16)
    w_kr = w_in0[:, n_h + Q_LORA + KV_LORA:]
    half = MLA_ROPE // 2
    w_kr_rot = jnp.concatenate([-w_kr[:, half:], w_kr[:, :half]], axis=1)
    zpad = jnp.zeros((d, LANES - MLA_ROPE), w_in0.dtype)
    wc = jnp.concatenate([w_in0[:, n_h:n_h + Q_LORA + KV_LORA], w_kr, zpad, w_kr_rot, zpad],
                         axis=1).astype(BF16)
    wqt = w_q_b[0].T.astype(BF16)
    w_kv = w_kv_b[0].reshape(KV_LORA, MLA_HEADS, MLA_NOPE + MLA_V)
    wkn = w_kv[:, :, :MLA_NOPE].reshape(KV_LORA, MLA_HEADS * MLA_NOPE).astype(BF16)
    wvt = w_kv[:, :, MLA_NOPE:].reshape(KV_LORA, MLA_WIDTH).T.astype(BF16)
    cos, sin = _rope_tables(seq)
    cos2 = jnp.concatenate([cos, cos], axis=1)
    sin2 = jnp.concatenate([sin, sin], axis=1)
    row = lambda a: a.reshape(1, -1)

    hq, hi, hf, hb, hg, qt, kc, vt = _in_proj_call(
        x, row(norm1_g[0]), wh, wc, row(q_a_norm_g[0]), row(kv_a_norm_g[0]), wqt, wkn, wvt,
        cos2, sin2, cos.T, sin.T, tm=t.proj_rows, scale=float(MLA_QK) ** -0.5 * math.log2(math.e))

    y_a = _hgrn_call(hq, hi, hf, hb, hg, lb_logits.reshape(4, HGRN_WIDTH), row(hgrn_norm_g[0]),
                     group=t.hgrn_group)
    o_b = _attn_call(qt, kc, vt, tq=t.attn_q, tk=t.attn_k)

    return _out_ffn_call(
        x, y_a, o_b, row(mla_norm_g[0]), w_out[0].astype(BF16), row(norm2_g[0]),
        w_gate[0].astype(BF16), w_up[0].astype(BF16), w_down[0].astype(BF16), row(final_norm_g),
        tm=t.ffn_rows, n_sub=t.ffn_sub)
```

```python
import functools
import math
from typing import NamedTuple

import jax
import jax.numpy as jnp
from jax import lax
from jax.experimental import pallas as pl
from jax.experimental.pallas import tpu as pltpu

EPS = 1e-6
ROPE_THETA = 10000.0
HGRN_HEADS = 8
HGRN_DK = 64
HGRN_WIDTH = HGRN_HEADS * HGRN_DK
HGRN_CHUNK = 64
PAIR = 2 * HGRN_DK
TRI_ROWS = 256
MLA_HEADS = 4
MLA_NOPE = 128
MLA_ROPE = 64
MLA_V = 128
MLA_QK = MLA_NOPE + MLA_ROPE
MLA_WIDTH = MLA_HEADS * MLA_V
Q_LORA = 384
KV_LORA = 256
LANES = 128
V7X_VMEM_BYTES = 64 * 2**20
VMEM_LIMIT = V7X_VMEM_BYTES * 7 // 8

BF16 = jnp.bfloat16
F32 = jnp.float32

_NT = (((1,), (1,)), ((), ()))
_TN = (((0,), (0,)), ((), ()))


def _dot(a, b):
    return jnp.dot(a, b, preferred_element_type=F32)


def _dot_nt(a, b):
    return lax.dot_general(a, b, _NT, preferred_element_type=F32)


def _dot_tn(a, b):
    return lax.dot_general(a, b, _TN, preferred_element_type=F32)


def _rms(x, g):
    return x * lax.rsqrt(jnp.mean(x * x, axis=-1, keepdims=True) + EPS) * g


def _sigmoid(x):
    return 1.0 / (1.0 + jnp.exp(-x))


def _split_bf16(x):
    hi = x.astype(BF16)
    lo = (x - hi.astype(F32)).astype(BF16)
    return hi, lo


def _in_proj_kernel(x_ref, g1_ref, wh_ref, wc_ref, gq_ref, gkv_ref, wqt_ref, wkn_ref, wvt_ref,
                    cos2_ref, sin2_ref, cost_ref, sint_ref,
                    hq_ref, hi_ref, hf_ref, hb_ref, hg_ref, qt_ref, kc_ref, vt_ref, *, scale):
    hn = _rms(x_ref[0], g1_ref[...]).astype(BF16)
    for g, o_ref in enumerate((hq_ref, hi_ref, hf_ref, hb_ref, hg_ref)):
        w = wh_ref[:, g * HGRN_WIDTH:(g + 1) * HGRN_WIDTH]
        o_ref[0] = _dot(hn, w).astype(o_ref.dtype)

    c = _dot(hn, wc_ref[...])
    c_q = c[:, :Q_LORA]
    c_kv = c[:, Q_LORA:Q_LORA + KV_LORA]
    kr0 = Q_LORA + KV_LORA
    k_r = c[:, kr0:kr0 + MLA_ROPE]
    k_rot = c[:, kr0 + LANES:kr0 + LANES + MLA_ROPE]
    cqn = _rms(c_q, gq_ref[...]).astype(BF16)
    ckvn = _rms(c_kv, gkv_ref[...]).astype(BF16)

    qt = _dot_nt(wqt_ref[...], cqn)
    cos_t = cost_ref[...]
    sin_t = sint_ref[...]
    half = MLA_ROPE // 2
    for h in range(MLA_HEADS):
        r0 = h * MLA_QK
        x1 = qt[r0 + MLA_NOPE:r0 + MLA_NOPE + half]
        x2 = qt[r0 + MLA_NOPE + half:r0 + MLA_QK]
        qt_ref[0, r0:r0 + MLA_NOPE, :] = (qt[r0:r0 + MLA_NOPE] * scale).astype(BF16)
        qt_ref[0, r0 + MLA_NOPE:r0 + MLA_NOPE + half, :] = (
            (x1 * cos_t - x2 * sin_t) * scale).astype(BF16)
        qt_ref[0, r0 + MLA_NOPE + half:r0 + MLA_QK, :] = (
            (x1 * sin_t + x2 * cos_t) * scale).astype(BF16)

    k_rope = (k_r * cos2_ref[...] + k_rot * sin2_ref[...]).astype(BF16)
    k_nope = _dot(ckvn, wkn_ref[...])
    for h in range(MLA_HEADS):
        kc_ref[0, h, :, :MLA_NOPE] = k_nope[:, h * MLA_NOPE:(h + 1) * MLA_NOPE].astype(BF16)
        kc_ref[0, h, :, MLA_NOPE:] = k_rope
    vt_ref[0] = _dot_nt(wvt_ref[...], ckvn).astype(BF16)


def _in_proj_call(x, g1, wh, wc, gq, gkv, wqt, wkn, wvt, cos2, sin2, cos_t, sin_t, *, tm, scale):
    bsz, seq, d = x.shape
    grid = (bsz, seq // tm)
    full = lambda a: pl.BlockSpec(a.shape, lambda b, i: (0,) * a.ndim, pipeline_mode=pl.Buffered(1))
    tok = lambda w: pl.BlockSpec((1, tm, w), lambda b, i: (b, i, 0))
    in_specs = [
        tok(d), full(g1), full(wh), full(wc), full(gq), full(gkv), full(wqt), full(wkn), full(wvt),
        pl.BlockSpec((tm, MLA_ROPE), lambda b, i: (i, 0)),
        pl.BlockSpec((tm, MLA_ROPE), lambda b, i: (i, 0)),
        pl.BlockSpec((MLA_ROPE // 2, tm), lambda b, i: (0, i)),
        pl.BlockSpec((MLA_ROPE // 2, tm), lambda b, i: (0, i)),
    ]
    hshape = jax.ShapeDtypeStruct((bsz, seq, HGRN_WIDTH), F32)
    out_shape = [hshape] * 5 + [
        jax.ShapeDtypeStruct((bsz, MLA_HEADS * MLA_QK, seq), BF16),
        jax.ShapeDtypeStruct((bsz, MLA_HEADS, seq, MLA_QK), BF16),
        jax.ShapeDtypeStruct((bsz, MLA_WIDTH, seq), BF16),
    ]
    out_specs = [tok(HGRN_WIDTH)] * 5 + [
        pl.BlockSpec((1, MLA_HEADS * MLA_QK, tm), lambda b, i: (b, 0, i)),
        pl.BlockSpec((1, MLA_HEADS, tm, MLA_QK), lambda b, i: (b, 0, i, 0)),
        pl.BlockSpec((1, MLA_WIDTH, tm), lambda b, i: (b, 0, i)),
    ]
    return pl.pallas_call(
        functools.partial(_in_proj_kernel, scale=scale),
        grid=grid, in_specs=in_specs, out_specs=out_specs, out_shape=out_shape,
        compiler_params=pltpu.CompilerParams(
            dimension_semantics=("parallel", "parallel"), vmem_limit_bytes=VMEM_LIMIT),
        name="in_proj",
    )(x, g1, wh, wc, gq, gkv, wqt, wkn, wvt, cos2, sin2, cos_t, sin_t)


def _hgrn_kernel(hq_ref, hi_ref, hf_ref, hb_ref, hg_ref, lbl_ref, gn_ref, y_ref, oacc_ref, *,
                 n_groups, group):
    c = HGRN_CHUNK
    rg = group * c
    row = lax.broadcasted_iota(jnp.int32, (TRI_ROWS, TRI_ROWS), 0)
    col = lax.broadcasted_iota(jnp.int32, (TRI_ROWS, TRI_ROWS), 1)
    same_chunk = (row // c) == (col // c)
    tri_fwd = (same_chunk & (col <= row)).astype(BF16)
    tri_bwd = (same_chunk & (col >= row)).astype(BF16)
    lane = lax.broadcasted_iota(jnp.int32, (c, PAIR), 1)
    srow = lax.broadcasted_iota(jnp.int32, (c, PAIR), 0)
    head0 = lane < HGRN_DK
    key_pos = jnp.where(head0, lane, lane - HGRN_DK)
    brow = lax.broadcasted_iota(jnp.int32, (PAIR, PAIR), 0)
    bcol = lax.broadcasted_iota(jnp.int32, (PAIR, PAIR), 1)
    same_head = (brow < HGRN_DK) == (bcol < HGRN_DK)
    head_mean = jnp.where(same_head, 1.0 / HGRN_DK, 0.0).astype(BF16)

    lbl = lbl_ref[...]

    def lower_bound(d):
        l0 = lbl[2 * d:2 * d + 1]
        l1 = lbl[2 * d + 1:2 * d + 2]
        m = jnp.maximum(l0, l1)
        e0 = jnp.exp(l0 - m)
        e1 = jnp.exp(l1 - m)
        return e0 / (e0 + e1)

    def sweep_group(n, st, gate_ref, lb, reverse):
        rows = pl.ds(pl.multiple_of(n * rg, rg), rg)
        hq2 = 0.5 * hq_ref[0, rows, :]
        q = hq2 + hq2 * jnp.tanh(hq2)
        v16 = hi_ref[0, rows, :].astype(BF16)
        b = 0.5 * (1.0 - lb)
        bt = b * jnp.tanh(0.5 * gate_ref[0, rows, :])
        log2_f = jnp.log2((0.5 * (1.0 + lb)) + bt)
        k = b - bt
        tri = tri_bwd if reverse else tri_fwd
        keep = (key_pos >= srow) if reverse else (key_pos <= srow)
        lf_hi, lf_lo = _split_bf16(log2_f)
        cum = jnp.concatenate(
            [_dot(tri, lf_hi[r:r + TRI_ROWS]) + _dot(tri, lf_lo[r:r + TRI_ROWS])
             for r in range(0, rg, TRI_ROWS)], axis=0)
        q_dec = (q * jnp.exp2(cum)).astype(BF16)
        k_inv = k * jnp.exp2(-cum)
        k_inv16 = k_inv.astype(BF16)
        zero16 = jnp.zeros((c, PAIR), BF16)
        order = range(group - 1, -1, -1) if reverse else range(group)
        sl = [slice(g * c, (g + 1) * c) for g in range(group)]
        decay, scores, u_t = {}, {}, {}
        for g in order:
            last = g * c if reverse else (g + 1) * c - 1
            decay[g] = jnp.exp2(cum[last:last + 1])
            k_end = (k_inv[sl[g]] * decay[g]).astype(BF16)
            ki = k_inv16[sl[g]]
            kk = jnp.concatenate([jnp.where(head0, ki, zero16), jnp.where(head0, zero16, ki)], axis=0)
            scores[g] = _dot_nt(q_dec[sl[g]], kk)
            u_t[g] = _dot_tn(v16[sl[g]], k_end)
        sts = {}
        for g in order:
            sts[g] = st.T.astype(BF16)
            st = st * decay[g] + jnp.where(same_head, u_t[g], 0.0)
        outs = [None] * group
        for g in order:
            vg = v16[sl[g]]
            vv = jnp.concatenate([jnp.where(head0, vg, zero16), jnp.where(head0, zero16, vg)], axis=0)
            p = jnp.where(keep, scores[g], 0.0).astype(BF16)
            outs[g] = _dot(p, vv) + _dot(q_dec[sl[g]], sts[g])
        return rows, jnp.concatenate(outs, axis=0), st

    st0 = jnp.zeros((PAIR, PAIR), F32)

    lb_f = lower_bound(0)

    def fwd_body(n, st):
        rows, o, st = sweep_group(n, st, hf_ref, lb_f, False)
        oacc_ref[rows, :] = o
        return st

    lax.fori_loop(0, n_groups, fwd_body, st0)

    lb_b = lower_bound(1)
    gn = gn_ref[...]

    def bwd_body(i, st):
        rows, o, st = sweep_group(n_groups - 1 - i, st, hb_ref, lb_b, True)
        o = o + oacc_ref[rows, :]
        ms = _dot((o * o).astype(BF16), head_mean)
        g2 = 0.5 * hg_ref[0, rows, :]
        gate = g2 + g2 * jnp.tanh(g2)
        y_ref[0, rows, :] = (o * lax.rsqrt(ms + EPS) * gn * gate).astype(y_ref.dtype)
        return st

    lax.fori_loop(0, n_groups, bwd_body, st0)


def _hgrn_call(hq, hi, hf, hb, hg, lbl, gn, *, group):
    bsz, seq, _ = hq.shape
    grid = (bsz, HGRN_WIDTH // PAIR)
    blk = pl.BlockSpec((1, seq, PAIR), lambda b, p: (b, 0, p))
    return pl.pallas_call(
        functools.partial(_hgrn_kernel, n_groups=seq // (HGRN_CHUNK * group), group=group),
        grid=grid,
        in_specs=[blk, blk, blk, blk, blk,
                  pl.BlockSpec((4, PAIR), lambda b, p: (0, p)),
                  pl.BlockSpec((1, PAIR), lambda b, p: (0, p))],
        out_specs=blk,
        out_shape=jax.ShapeDtypeStruct((bsz, seq, HGRN_WIDTH), BF16),
        scratch_shapes=[pltpu.VMEM((seq, PAIR), F32)],
        compiler_params=pltpu.CompilerParams(
            dimension_semantics=("parallel", "parallel"), vmem_limit_bytes=VMEM_LIMIT),
        name="hgrn2",
    )(hq, hi, hf, hb, hg, lbl, gn)


def _attn_kernel(qt_ref, kc_ref, vt_ref, o_ref, s0_ref, s1_ref, m0_ref, m1_ref, acc_ref, *, tk, tq):
    r = pl.program_id(0)
    seq = kc_ref.shape[2]
    n_qblocks = qt_ref.shape[2] // tq

    @pl.when(r == 0)
    def _():
        s1_ref[...] = jnp.zeros_like(s1_ref)
        m1_ref[...] = jnp.zeros_like(m1_ref)

    def step(s_cur, s_prev, m_cur_ref, m_prev_ref):
        for j in range(n_qblocks):
            qs = slice(j * tq, (j + 1) * tq)
            q_t = qt_ref[0, :, qs]
            m_prev = m_prev_ref[:, qs]
            m_cur = None
            l = None
            for c in range(seq // tk):
                ks = slice(c * tk, (c + 1) * tk)
                s = _dot(kc_ref[0, 0, ks, :], q_t)
                cm = jnp.max(s, axis=0, keepdims=True)
                m_cur = cm if m_cur is None else jnp.maximum(m_cur, cm)
                p = jnp.exp2(s_prev[ks, qs] - m_prev)
                s_cur[ks, qs] = s
                ps = jnp.sum(p, axis=0, keepdims=True)
                l = ps if l is None else l + ps
                pv = _dot(vt_ref[0, :, ks], p.astype(BF16))
                if c == 0:
                    acc_ref[...] = pv
                else:
                    acc_ref[...] += pv
            m_cur_ref[:, qs] = m_cur
            o_ref[0, qs, :] = (acc_ref[...] / l).T

    @pl.when(r % 2 == 0)
    def _():
        step(s0_ref, s1_ref, m0_ref, m1_ref)

    @pl.when(r % 2 == 1)
    def _():
        step(s1_ref, s0_ref, m1_ref, m0_ref)


def _attn_call(qt, kc, vt, *, tq, tk, n_qblocks):
    bsz, _, seq = qt.shape
    tqs = tq * n_qblocks
    nq = seq // tqs
    n_prob = bsz * MLA_HEADS * nq

    def prob(r):
        r = jnp.minimum(r, n_prob - 1)
        return r // (MLA_HEADS * nq), (r // nq) % MLA_HEADS, r % nq

    def lag(r):
        return prob(jnp.maximum(r - 1, 0))

    return pl.pallas_call(
        functools.partial(_attn_kernel, tk=tk, tq=tq),
        grid=(n_prob + 1,),
        in_specs=[
            pl.BlockSpec((1, MLA_QK, tqs), lambda r: prob(r)),
            pl.BlockSpec((1, 1, seq, MLA_QK), lambda r: prob(r)[:2] + (0, 0)),
            pl.BlockSpec((1, MLA_V, seq), lambda r: lag(r)[:2] + (0,)),
        ],
        out_specs=pl.BlockSpec((1, tqs, MLA_V), lambda r: (lag(r)[0], lag(r)[2], lag(r)[1])),
        out_shape=jax.ShapeDtypeStruct((bsz, seq, MLA_WIDTH), F32),
        scratch_shapes=[pltpu.VMEM((seq, tqs), F32), pltpu.VMEM((seq, tqs), F32),
                        pltpu.VMEM((1, tqs), F32), pltpu.VMEM((1, tqs), F32),
                        pltpu.VMEM((MLA_V, tq), F32)],
        compiler_params=pltpu.CompilerParams(
            dimension_semantics=("arbitrary",), vmem_limit_bytes=VMEM_LIMIT),
        name="mla_attn",
    )(qt, kc, vt)


def _out_ffn_kernel(x_ref, ya_ref, ob_ref, gm_ref, wo_ref, g2_ref, wg_ref, wu_ref, wd_ref, gf_ref,
                    out_ref, *, n_sub):
    tm = x_ref.shape[1]
    ts = tm // n_sub
    subs = [slice(i * ts, (i + 1) * ts) for i in range(n_sub)]
    x1 = [x_ref[0, r, :] + _dot(ya_ref[0, r, :], wo_ref[:HGRN_WIDTH, :])
          + _dot(_rms(ob_ref[0, r, :], gm_ref[...]).astype(BF16), wo_ref[HGRN_WIDTH:, :]) for r in subs]
    h2 = [_rms(v, g2_ref[...]).astype(BF16) for v in x1]
    gate = [_dot(h, wg_ref[...]) for h in h2]
    up = [_dot(h, wu_ref[...]) for h in h2]
    a = [(g * _sigmoid(g) * u).astype(BF16) for g, u in zip(gate, up)]
    for r, v, ai in zip(subs, x1, a):
        out_ref[0, r, :] = _rms(v + _dot(ai, wd_ref[...]), gf_ref[...])


def _out_ffn_call(x, ya, ob, gm, wo, g2, wg, wu, wd, gf, *, tm, n_sub):
    bsz, seq, d = x.shape
    grid = (bsz, seq // tm)
    tok = lambda w: pl.BlockSpec((1, tm, w), lambda b, i: (b, i, 0))
    full = lambda a: pl.BlockSpec(a.shape, lambda b, i: (0,) * a.ndim,
                                  pipeline_mode=pl.Buffered(1))
    return pl.pallas_call(
        functools.partial(_out_ffn_kernel, n_sub=n_sub),
        grid=grid,
        in_specs=[tok(d), tok(HGRN_WIDTH), tok(MLA_WIDTH), full(gm), full(wo), full(g2),
                  full(wg), full(wu), full(wd), full(gf)],
        out_specs=tok(d),
        out_shape=jax.ShapeDtypeStruct((bsz, seq, d), x.dtype),
        compiler_params=pltpu.CompilerParams(
            dimension_semantics=("parallel", "parallel"), vmem_limit_bytes=VMEM_LIMIT),
        name="out_ffn",
    )(x, ya, ob, gm, wo, g2, wg, wu, wd, gf)


class _Tiles(NamedTuple):
    proj_rows: int
    hgrn_group: int
    attn_q: int
    attn_qblocks: int
    attn_k: int
    ffn_rows: int
    ffn_sub: int


def _tiles(seq):
    rows = min(512, seq)
    return _Tiles(proj_rows=min(2 * rows, seq), hgrn_group=min(32, seq // HGRN_CHUNK),
                  attn_q=rows, attn_qblocks=min(2, seq // rows), attn_k=rows, ffn_rows=rows, ffn_sub=2)


def _rope_tables(seq):
    inv = 1.0 / (ROPE_THETA ** (jnp.arange(0, MLA_ROPE, 2, dtype=F32) / MLA_ROPE))
    ang = jnp.arange(seq, dtype=F32)[:, None] * inv[None, :]
    return jnp.cos(ang), jnp.sin(ang)


def kernel(x, norm1_g, w_in, lb_logits, hgrn_norm_g, q_a_norm_g, w_q_b, kv_a_norm_g, w_kv_b,
           mla_norm_g, w_out, norm2_g, w_gate, w_up, w_down, final_norm_g):
    bsz, seq, d = x.shape
    assert w_in.shape[0] == 1 and lb_logits.shape[1] == 2, "single-layer trunk only"
    t = _tiles(seq)
    n_h = 5 * HGRN_WIDTH

    w_in0 = w_in[0]
    wh = w_in0[:, :n_h].astype(BF16)
    w_kr = w_in0[:, n_h + Q_LORA + KV_LORA:]
    half = MLA_ROPE // 2
    w_kr_rot = jnp.concatenate([-w_kr[:, half:], w_kr[:, :half]], axis=1)
    zpad = jnp.zeros((d, LANES - MLA_ROPE), w_in0.dtype)
    wc = jnp.concatenate([w_in0[:, n_h:n_h + Q_LORA + KV_LORA], w_kr, zpad, w_kr_rot, zpad],
                         axis=1).astype(BF16)
    wqt = w_q_b[0].T.astype(BF16)
    w_kv = w_kv_b[0].reshape(KV_LORA, MLA_HEADS, MLA_NOPE + MLA_V)
    wkn = w_kv[:, :, :MLA_NOPE].reshape(KV_LORA, MLA_HEADS * MLA_NOPE).astype(BF16)
    wvt = w_kv[:, :, MLA_NOPE:].reshape(KV_LORA, MLA_WIDTH).T.astype(BF16)
    cos, sin = _rope_tables(seq)
    cos2 = jnp.concatenate([cos, cos], axis=1)
    sin2 = jnp.concatenate([sin, sin], axis=1)
    row = lambda a: a.reshape(1, -1)

    hq, hi, hf, hb, hg, qt, kc, vt = _in_proj_call(
        x, row(norm1_g[0]), wh, wc, row(q_a_norm_g[0]), row(kv_a_norm_g[0]), wqt, wkn, wvt,
        cos2, sin2, cos.T, sin.T, tm=t.proj_rows, scale=float(MLA_QK) ** -0.5 * math.log2(math.e))

    y_a = _hgrn_call(hq, hi, hf, hb, hg, lb_logits.reshape(4, HGRN_WIDTH), row(hgrn_norm_g[0]),
                     group=t.hgrn_group)
    o_b = _attn_call(qt, kc, vt, tq=t.attn_q, tk=t.attn_k, n_qblocks=t.attn_qblocks)

    return _out_ffn_call(
        x, y_a, o_b, row(mla_norm_g[0]), w_out[0].astype(BF16), row(norm2_g[0]),
        w_gate[0].astype(BF16), w_up[0].astype(BF16), w_down[0].astype(BF16), row(final_norm_g),
        tm=t.ffn_rows, n_sub=t.ffn_sub)
```

```python
import functools
import math
from typing import NamedTuple

import jax
import jax.numpy as jnp
from jax import lax
from jax.experimental import pallas as pl
from jax.experimental.pallas import tpu as pltpu

EPS = 1e-6
ROPE_THETA = 10000.0
HGRN_HEADS = 8
HGRN_DK = 64
HGRN_WIDTH = HGRN_HEADS * HGRN_DK
HGRN_CHUNK = 64
HGRN_BLOCK = 2 * HGRN_CHUNK
PAIR = 2 * HGRN_DK
TRI_ROWS = 256
MLA_HEADS = 4
MLA_NOPE = 128
MLA_ROPE = 64
MLA_V = 128
MLA_QK = MLA_NOPE + MLA_ROPE
MLA_WIDTH = MLA_HEADS * MLA_V
Q_LORA = 384
KV_LORA = 256
LANES = 128
V7X_VMEM_BYTES = 64 * 2**20
VMEM_LIMIT = V7X_VMEM_BYTES * 7 // 8

BF16 = jnp.bfloat16
F32 = jnp.float32

_NT = (((1,), (1,)), ((), ()))
_TN = (((0,), (0,)), ((), ()))


def _dot(a, b):
    return jnp.dot(a, b, preferred_element_type=F32)


def _dot_nt(a, b):
    return lax.dot_general(a, b, _NT, preferred_element_type=F32)


def _dot_tn(a, b):
    return lax.dot_general(a, b, _TN, preferred_element_type=F32)


def _rms(x, g):
    return x * lax.rsqrt(jnp.mean(x * x, axis=-1, keepdims=True) + EPS) * g


def _sigmoid(x):
    return 1.0 / (1.0 + jnp.exp(-x))


def _split_bf16(x):
    hi = x.astype(BF16)
    lo = (x - hi.astype(F32)).astype(BF16)
    return hi, lo


def _in_proj_kernel(x_ref, g1_ref, wh_ref, wc_ref, gq_ref, gkv_ref, wqt_ref, wkn_ref, wvt_ref,
                    cos2_ref, sin2_ref, cost_ref, sint_ref,
                    hq_ref, hi_ref, hf_ref, hb_ref, hg_ref, qt_ref, kc_ref, vt_ref, *, scale):
    hn = _rms(x_ref[0], g1_ref[...]).astype(BF16)
    for g, o_ref in enumerate((hq_ref, hi_ref, hf_ref, hb_ref, hg_ref)):
        w = wh_ref[:, g * HGRN_WIDTH:(g + 1) * HGRN_WIDTH]
        o_ref[0] = _dot(hn, w).astype(o_ref.dtype)

    c = _dot(hn, wc_ref[...])
    c_q = c[:, :Q_LORA]
    c_kv = c[:, Q_LORA:Q_LORA + KV_LORA]
    kr0 = Q_LORA + KV_LORA
    k_r = c[:, kr0:kr0 + MLA_ROPE]
    k_rot = c[:, kr0 + LANES:kr0 + LANES + MLA_ROPE]
    cqn = _rms(c_q, gq_ref[...]).astype(BF16)
    ckvn = _rms(c_kv, gkv_ref[...]).astype(BF16)

    qt = _dot_nt(wqt_ref[...], cqn)
    cos_t = cost_ref[...]
    sin_t = sint_ref[...]
    half = MLA_ROPE // 2
    for h in range(MLA_HEADS):
        r0 = h * MLA_QK
        x1 = qt[r0 + MLA_NOPE:r0 + MLA_NOPE + half]
        x2 = qt[r0 + MLA_NOPE + half:r0 + MLA_QK]
        qt_ref[0, r0:r0 + MLA_NOPE, :] = (qt[r0:r0 + MLA_NOPE] * scale).astype(BF16)
        qt_ref[0, r0 + MLA_NOPE:r0 + MLA_NOPE + half, :] = (
            (x1 * cos_t - x2 * sin_t) * scale).astype(BF16)
        qt_ref[0, r0 + MLA_NOPE + half:r0 + MLA_QK, :] = (
            (x1 * sin_t + x2 * cos_t) * scale).astype(BF16)

    k_rope = (k_r * cos2_ref[...] + k_rot * sin2_ref[...]).astype(BF16)
    k_nope = _dot(ckvn, wkn_ref[...])
    for h in range(MLA_HEADS):
        kc_ref[0, h, :, :MLA_NOPE] = k_nope[:, h * MLA_NOPE:(h + 1) * MLA_NOPE].astype(BF16)
        kc_ref[0, h, :, MLA_NOPE:] = k_rope
    vt_ref[0] = _dot_nt(wvt_ref[...], ckvn).astype(BF16)


def _in_proj_call(x, g1, wh, wc, gq, gkv, wqt, wkn, wvt, cos2, sin2, cos_t, sin_t, *, tm, scale):
    bsz, seq, d = x.shape
    grid = (bsz, seq // tm)
    full = lambda a: pl.BlockSpec(a.shape, lambda b, i: (0,) * a.ndim, pipeline_mode=pl.Buffered(1))
    tok = lambda w: pl.BlockSpec((1, tm, w), lambda b, i: (b, i, 0))
    in_specs = [
        tok(d), full(g1), full(wh), full(wc), full(gq), full(gkv), full(wqt), full(wkn), full(wvt),
        pl.BlockSpec((tm, MLA_ROPE), lambda b, i: (i, 0)),
        pl.BlockSpec((tm, MLA_ROPE), lambda b, i: (i, 0)),
        pl.BlockSpec((MLA_ROPE // 2, tm), lambda b, i: (0, i)),
        pl.BlockSpec((MLA_ROPE // 2, tm), lambda b, i: (0, i)),
    ]
    hshape = jax.ShapeDtypeStruct((bsz, seq, HGRN_WIDTH), F32)
    out_shape = [hshape] * 5 + [
        jax.ShapeDtypeStruct((bsz, MLA_HEADS * MLA_QK, seq), BF16),
        jax.ShapeDtypeStruct((bsz, MLA_HEADS, seq, MLA_QK), BF16),
        jax.ShapeDtypeStruct((bsz, MLA_WIDTH, seq), BF16),
    ]
    out_specs = [tok(HGRN_WIDTH)] * 5 + [
        pl.BlockSpec((1, MLA_HEADS * MLA_QK, tm), lambda b, i: (b, 0, i)),
        pl.BlockSpec((1, MLA_HEADS, tm, MLA_QK), lambda b, i: (b, 0, i, 0)),
        pl.BlockSpec((1, MLA_WIDTH, tm), lambda b, i: (b, 0, i)),
    ]
    return pl.pallas_call(
        functools.partial(_in_proj_kernel, scale=scale),
        grid=grid, in_specs=in_specs, out_specs=out_specs, out_shape=out_shape,
        compiler_params=pltpu.CompilerParams(
            dimension_semantics=("parallel", "parallel"), vmem_limit_bytes=VMEM_LIMIT),
        name="in_proj",
    )(x, g1, wh, wc, gq, gkv, wqt, wkn, wvt, cos2, sin2, cos_t, sin_t)


def _hgrn_kernel(hq_ref, hi_ref, hf_ref, hb_ref, hg_ref, lbl_ref, gn_ref, y_ref, oacc_ref, *,
                 n_groups, group):
    c = HGRN_BLOCK
    rg = group * c
    row = lax.broadcasted_iota(jnp.int32, (TRI_ROWS, TRI_ROWS), 0)
    col = lax.broadcasted_iota(jnp.int32, (TRI_ROWS, TRI_ROWS), 1)
    same_block = (row // c) == (col // c)
    tri_fwd = (same_block & (col <= row)).astype(BF16)
    tri_bwd = (same_block & (col >= row)).astype(BF16)
    lane = lax.broadcasted_iota(jnp.int32, (c, PAIR), 1)
    head0 = lane < HGRN_DK
    slane = lax.broadcasted_iota(jnp.int32, (c, 2 * c), 1)
    srow = lax.broadcasted_iota(jnp.int32, (c, 2 * c), 0)
    key_pos = jnp.where(slane < c, slane, slane - c)
    brow = lax.broadcasted_iota(jnp.int32, (PAIR, PAIR), 0)
    bcol = lax.broadcasted_iota(jnp.int32, (PAIR, PAIR), 1)
    same_head = (brow < HGRN_DK) == (bcol < HGRN_DK)
    head_mean = jnp.where(same_head, 1.0 / HGRN_DK, 0.0).astype(BF16)

    lbl = lbl_ref[...]

    def lower_bound(d):
        l0 = lbl[2 * d:2 * d + 1]
        l1 = lbl[2 * d + 1:2 * d + 2]
        m = jnp.maximum(l0, l1)
        e0 = jnp.exp(l0 - m)
        e1 = jnp.exp(l1 - m)
        return e0 / (e0 + e1)

    def sweep_group(n, st, gate_ref, lb, reverse):
        rows = pl.ds(pl.multiple_of(n * rg, rg), rg)
        hq2 = 0.5 * hq_ref[0, rows, :]
        q = hq2 + hq2 * jnp.tanh(hq2)
        v16 = hi_ref[0, rows, :].astype(BF16)
        b = 0.5 * (1.0 - lb)
        bt = b * jnp.tanh(0.5 * gate_ref[0, rows, :])
        log2_f = jnp.log2((0.5 * (1.0 + lb)) + bt)
        k = b - bt
        tri = tri_bwd if reverse else tri_fwd
        keep = (key_pos >= srow) if reverse else (key_pos <= srow)
        lf_hi, lf_lo = _split_bf16(log2_f)
        cum = jnp.concatenate(
            [_dot(tri, lf_hi[r:r + TRI_ROWS]) + _dot(tri, lf_lo[r:r + TRI_ROWS])
             for r in range(0, rg, TRI_ROWS)], axis=0)
        zero16 = jnp.zeros((c, PAIR), BF16)
        order = range(group - 1, -1, -1) if reverse else range(group)
        sl = [slice(g * c, (g + 1) * c) for g in range(group)]
        mid = c // 2 if reverse else c // 2 - 1
        q_dec, decay, scores, u_t = {}, {}, {}, {}
        for g in order:
            cg = cum[sl[g]]
            cm = cg[mid:mid + 1]
            total = cg[0:1] if reverse else cg[c - 1:c]
            q_mid = q[sl[g]] * jnp.exp2(cg - cm)
            k_mid = k[sl[g]] * jnp.exp2(cm - cg)
            q_dec[g] = (q_mid * jnp.exp2(cm)).astype(BF16)
            k_end = (k_mid * jnp.exp2(total - cm)).astype(BF16)
            decay[g] = jnp.exp2(total)
            ki = k_mid.astype(BF16)
            kk = jnp.concatenate([jnp.where(head0, ki, zero16), jnp.where(head0, zero16, ki)], axis=0)
            scores[g] = _dot_nt(q_mid.astype(BF16), kk)
            u_t[g] = _dot_tn(v16[sl[g]], k_end)
        sts = {}
        for g in order:
            sts[g] = st.T.astype(BF16)
            st = st * decay[g] + jnp.where(same_head, u_t[g], 0.0)
        outs = [None] * group
        for g in order:
            vg = v16[sl[g]]
            vv = jnp.concatenate([jnp.where(head0, vg, zero16), jnp.where(head0, zero16, vg)], axis=0)
            p = jnp.where(keep, scores[g], 0.0).astype(BF16)
            outs[g] = _dot(p, vv) + _dot(q_dec[g], sts[g])
        return rows, jnp.concatenate(outs, axis=0), st

    st0 = jnp.zeros((PAIR, PAIR), F32)

    lb_f = lower_bound(0)

    def fwd_body(n, st):
        rows, o, st = sweep_group(n, st, hf_ref, lb_f, False)
        oacc_ref[rows, :] = o
        return st

    lax.fori_loop(0, n_groups, fwd_body, st0)

    lb_b = lower_bound(1)
    gn = gn_ref[...]

    def bwd_body(i, st):
        rows, o, st = sweep_group(n_groups - 1 - i, st, hb_ref, lb_b, True)
        o = o + oacc_ref[rows, :]
        ms = _dot((o * o).astype(BF16), head_mean)
        g2 = 0.5 * hg_ref[0, rows, :]
        gate = g2 + g2 * jnp.tanh(g2)
        y_ref[0, rows, :] = (o * lax.rsqrt(ms + EPS) * gn * gate).astype(y_ref.dtype)
        return st

    lax.fori_loop(0, n_groups, bwd_body, st0)


def _hgrn_call(hq, hi, hf, hb, hg, lbl, gn, *, group):
    bsz, seq, _ = hq.shape
    grid = (bsz, HGRN_WIDTH // PAIR)
    blk = pl.BlockSpec((1, seq, PAIR), lambda b, p: (b, 0, p))
    return pl.pallas_call(
        functools.partial(_hgrn_kernel, n_groups=seq // (HGRN_BLOCK * group), group=group),
        grid=grid,
        in_specs=[blk, blk, blk, blk, blk,
                  pl.BlockSpec((4, PAIR), lambda b, p: (0, p)),
                  pl.BlockSpec((1, PAIR), lambda b, p: (0, p))],
        out_specs=blk,
        out_shape=jax.ShapeDtypeStruct((bsz, seq, HGRN_WIDTH), BF16),
        scratch_shapes=[pltpu.VMEM((seq, PAIR), F32)],
        compiler_params=pltpu.CompilerParams(
            dimension_semantics=("parallel", "parallel"), vmem_limit_bytes=VMEM_LIMIT),
        name="hgrn2",
    )(hq, hi, hf, hb, hg, lbl, gn)


def _attn_kernel(qt_ref, kc_ref, vt_ref, o_ref, s0_ref, s1_ref, m0_ref, m1_ref, acc_ref, *, tk, tq):
    r = pl.program_id(0)
    seq = kc_ref.shape[2]
    n_qblocks = qt_ref.shape[2] // tq

    @pl.when(r == 0)
    def _():
        s1_ref[...] = jnp.zeros_like(s1_ref)
        m1_ref[...] = jnp.zeros_like(m1_ref)

    def step(s_cur, s_prev, m_cur_ref, m_prev_ref):
        for j in range(n_qblocks):
            qs = slice(j * tq, (j + 1) * tq)
            q_t = qt_ref[0, :, qs]
            m_prev = m_prev_ref[:, qs]
            m_cur = None
            l = None
            for c in range(seq // tk):
                ks = slice(c * tk, (c + 1) * tk)
                s = _dot(kc_ref[0, 0, ks, :], q_t)
                cm = jnp.max(s, axis=0, keepdims=True)
                m_cur = cm if m_cur is None else jnp.maximum(m_cur, cm)
                p = jnp.exp2(s_prev[ks, qs] - m_prev)
                s_cur[ks, qs] = s
                ps = jnp.sum(p, axis=0, keepdims=True)
                l = ps if l is None else l + ps
                pv = _dot(vt_ref[0, :, ks], p.astype(BF16))
                if c == 0:
                    acc_ref[...] = pv
                else:
                    acc_ref[...] += pv
            m_cur_ref[:, qs] = m_cur
            o_ref[0, qs, :] = (acc_ref[...] / l).T

    @pl.when(r % 2 == 0)
    def _():
        step(s0_ref, s1_ref, m0_ref, m1_ref)

    @pl.when(r % 2 == 1)
    def _():
        step(s1_ref, s0_ref, m1_ref, m0_ref)


def _attn_call(qt, kc, vt, *, tq, tk, n_qblocks):
    bsz, _, seq = qt.shape
    tqs = tq * n_qblocks
    nq = seq // tqs
    n_prob = bsz * MLA_HEADS * nq

    def prob(r):
        r = jnp.minimum(r, n_prob - 1)
        return r // (MLA_HEADS * nq), (r // nq) % MLA_HEADS, r % nq

    def lag(r):
        return prob(jnp.maximum(r - 1, 0))

    return pl.pallas_call(
        functools.partial(_attn_kernel, tk=tk, tq=tq),
        grid=(n_prob + 1,),
        in_specs=[
            pl.BlockSpec((1, MLA_QK, tqs), lambda r: prob(r)),
            pl.BlockSpec((1, 1, seq, MLA_QK), lambda r: prob(r)[:2] + (0, 0)),
            pl.BlockSpec((1, MLA_V, seq), lambda r: lag(r)[:2] + (0,)),
        ],
        out_specs=pl.BlockSpec((1, tqs, MLA_V), lambda r: (lag(r)[0], lag(r)[2], lag(r)[1])),
        out_shape=jax.ShapeDtypeStruct((bsz, seq, MLA_WIDTH), F32),
        scratch_shapes=[pltpu.VMEM((seq, tqs), F32), pltpu.VMEM((seq, tqs), F32),
                        pltpu.VMEM((1, tqs), F32), pltpu.VMEM((1, tqs), F32),
                        pltpu.VMEM((MLA_V, tq), F32)],
        compiler_params=pltpu.CompilerParams(
            dimension_semantics=("arbitrary",), vmem_limit_bytes=VMEM_LIMIT),
        name="mla_attn",
    )(qt, kc, vt)


def _out_ffn_kernel(x_ref, ya_ref, ob_ref, gm_ref, wo_ref, g2_ref, wg_ref, wu_ref, wd_ref, gf_ref,
                    out_ref, *, n_sub):
    tm = x_ref.shape[1]
    ts = tm // n_sub
    subs = [slice(i * ts, (i + 1) * ts) for i in range(n_sub)]
    x1 = [x_ref[0, r, :] + _dot(ya_ref[0, r, :], wo_ref[:HGRN_WIDTH, :])
          + _dot(_rms(ob_ref[0, r, :], gm_ref[...]).astype(BF16), wo_ref[HGRN_WIDTH:, :]) for r in subs]
    h2 = [_rms(v, g2_ref[...]).astype(BF16) for v in x1]
    gate = [_dot(h, wg_ref[...]) for h in h2]
    up = [_dot(h, wu_ref[...]) for h in h2]
    a = [(g * _sigmoid(g) * u).astype(BF16) for g, u in zip(gate, up)]
    for r, v, ai in zip(subs, x1, a):
        out_ref[0, r, :] = _rms(v + _dot(ai, wd_ref[...]), gf_ref[...])


def _out_ffn_call(x, ya, ob, gm, wo, g2, wg, wu, wd, gf, *, tm, n_sub):
    bsz, seq, d = x.shape
    grid = (bsz, seq // tm)
    tok = lambda w: pl.BlockSpec((1, tm, w), lambda b, i: (b, i, 0))
    full = lambda a: pl.BlockSpec(a.shape, lambda b, i: (0,) * a.ndim,
                                  pipeline_mode=pl.Buffered(1))
    return pl.pallas_call(
        functools.partial(_out_ffn_kernel, n_sub=n_sub),
        grid=grid,
        in_specs=[tok(d), tok(HGRN_WIDTH), tok(MLA_WIDTH), full(gm), full(wo), full(g2),
                  full(wg), full(wu), full(wd), full(gf)],
        out_specs=tok(d),
        out_shape=jax.ShapeDtypeStruct((bsz, seq, d), x.dtype),
        compiler_params=pltpu.CompilerParams(
            dimension_semantics=("parallel", "parallel"), vmem_limit_bytes=VMEM_LIMIT),
        name="out_ffn",
    )(x, ya, ob, gm, wo, g2, wg, wu, wd, gf)


class _Tiles(NamedTuple):
    proj_rows: int
    hgrn_group: int
    attn_q: int
    attn_qblocks: int
    attn_k: int
    ffn_rows: int
    ffn_sub: int


def _tiles(seq):
    rows = min(512, seq)
    return _Tiles(proj_rows=min(2 * rows, seq), hgrn_group=min(16, seq // HGRN_BLOCK),
                  attn_q=rows, attn_qblocks=min(2, seq // rows), attn_k=rows, ffn_rows=rows, ffn_sub=2)


def _rope_tables(seq):
    inv = 1.0 / (ROPE_THETA ** (jnp.arange(0, MLA_ROPE, 2, dtype=F32) / MLA_ROPE))
    ang = jnp.arange(seq, dtype=F32)[:, None] * inv[None, :]
    return jnp.cos(ang), jnp.sin(ang)


def kernel(x, norm1_g, w_in, lb_logits, hgrn_norm_g, q_a_norm_g, w_q_b, kv_a_norm_g, w_kv_b,
           mla_norm_g, w_out, norm2_g, w_gate, w_up, w_down, final_norm_g):
    bsz, seq, d = x.shape
    assert w_in.shape[0] == 1 and lb_logits.shape[1] == 2, "single-layer trunk only"
    t = _tiles(seq)
    n_h = 5 * HGRN_WIDTH

    w_in0 = w_in[0]
    wh = w_in0[:, :n_h].astype(BF16)
    w_kr = w_in0[:, n_h + Q_LORA + KV_LORA:]
    half = MLA_ROPE // 2
    w_kr_rot = jnp.concatenate([-w_kr[:, half:], w_kr[:, :half]], axis=1)
    zpad = jnp.zeros((d, LANES - MLA_ROPE), w_in0.dtype)
    wc = jnp.concatenate([w_in0[:, n_h:n_h + Q_LORA + KV_LORA], w_kr, zpad, w_kr_rot, zpad],
                         axis=1).astype(BF16)
    wqt = w_q_b[0].T.astype(BF16)
    w_kv = w_kv_b[0].reshape(KV_LORA, MLA_HEADS, MLA_NOPE + MLA_V)
    wkn = w_kv[:, :, :MLA_NOPE].reshape(KV_LORA, MLA_HEADS * MLA_NOPE).astype(BF16)
    wvt = w_kv[:, :, MLA_NOPE:].reshape(KV_LORA, MLA_WIDTH).T.astype(BF16)
    cos, sin = _rope_tables(seq)
    cos2 = jnp.concatenate([cos, cos], axis=1)
    sin2 = jnp.concatenate([sin, sin], axis=1)
    row = lambda a: a.reshape(1, -1)

    hq, hi, hf, hb, hg, qt, kc, vt = _in_proj_call(
        x, row(norm1_g[0]), wh, wc, row(q_a_norm_g[0]), row(kv_a_norm_g[0]), wqt, wkn, wvt,
        cos2, sin2, cos.T, sin.T, tm=t.proj_rows, scale=float(MLA_QK) ** -0.5 * math.log2(math.e))

    y_a = _hgrn_call(hq, hi, hf, hb, hg, lb_logits.reshape(4, HGRN_WIDTH), row(hgrn_norm_g[0]),
                     group=t.hgrn_group)
    o_b = _attn_call(qt, kc, vt, tq=t.attn_q, tk=t.attn_k, n_qblocks=t.attn_qblocks)

    return _out_ffn_call(
        x, y_a, o_b, row(mla_norm_g[0]), w_out[0].astype(BF16), row(norm2_g[0]),
        w_gate[0].astype(BF16), w_up[0].astype(BF16), w_down[0].astype(BF16), row(final_norm_g),
        tm=t.ffn_rows, n_sub=t.ffn_sub)
```

```python
import functools
import math
from typing import NamedTuple

import jax
import jax.numpy as jnp
from jax import lax
from jax.experimental import pallas as pl
from jax.experimental.pallas import tpu as pltpu

EPS = 1e-6
ROPE_THETA = 10000.0
HGRN_HEADS = 8
HGRN_DK = 64
HGRN_WIDTH = HGRN_HEADS * HGRN_DK
HGRN_CHUNK = 64
HGRN_BLOCK = 2 * HGRN_CHUNK
PAIR = 2 * HGRN_DK
TRI_ROWS = 256
MLA_HEADS = 4
MLA_NOPE = 128
MLA_ROPE = 64
MLA_V = 128
MLA_QK = MLA_NOPE + MLA_ROPE
MLA_WIDTH = MLA_HEADS * MLA_V
Q_LORA = 384
KV_LORA = 256
V7X_VMEM_BYTES = 64 * 2**20
VMEM_LIMIT = V7X_VMEM_BYTES * 7 // 8

BF16 = jnp.bfloat16
F32 = jnp.float32

_NT = (((1,), (1,)), ((), ()))
_TN = (((0,), (0,)), ((), ()))


def _dot(a, b):
    return jnp.dot(a, b, preferred_element_type=F32)


def _dot_nt(a, b):
    return lax.dot_general(a, b, _NT, preferred_element_type=F32)


def _dot_tn(a, b):
    return lax.dot_general(a, b, _TN, preferred_element_type=F32)


def _rms(x, g):
    return x * lax.rsqrt(jnp.mean(x * x, axis=-1, keepdims=True) + EPS) * g


def _sigmoid(x):
    return 1.0 / (1.0 + jnp.exp(-x))


def _split_bf16(x):
    hi = x.astype(BF16)
    lo = (x - hi.astype(F32)).astype(BF16)
    return hi, lo


def _in_proj_kernel(x_ref, g1_ref, wh_ref, wc_ref, gq_ref, gkv_ref, wqt_ref, wkn_ref, wvt_ref,
                    cs_ref, cost_ref, sint_ref,
                    hq_ref, hi_ref, hf_ref, hb_ref, hg_ref, qt_ref, kc_ref, vt_ref, *, scale):
    hn = _rms(x_ref[0], g1_ref[...]).astype(BF16)
    for g, o_ref in enumerate((hq_ref, hi_ref, hf_ref, hb_ref, hg_ref)):
        w = wh_ref[:, g * HGRN_WIDTH:(g + 1) * HGRN_WIDTH]
        o_ref[0] = _dot(hn, w).astype(o_ref.dtype)

    c = _dot(hn, wc_ref[...])
    c_q = c[:, :Q_LORA]
    c_kv = c[:, Q_LORA:Q_LORA + KV_LORA]
    kr0 = Q_LORA + KV_LORA
    k_pair = c[:, kr0:kr0 + 2 * MLA_ROPE]
    cqn = _rms(c_q, gq_ref[...]).astype(BF16)
    ckvn = _rms(c_kv, gkv_ref[...]).astype(BF16)

    qt = _dot_nt(wqt_ref[...], cqn)
    cos_t = cost_ref[...]
    sin_t = sint_ref[...]
    half = MLA_ROPE // 2
    for h in range(MLA_HEADS):
        r0 = h * MLA_QK
        x1 = qt[r0 + MLA_NOPE:r0 + MLA_NOPE + half]
        x2 = qt[r0 + MLA_NOPE + half:r0 + MLA_QK]
        qt_ref[0, r0:r0 + MLA_NOPE, :] = (qt[r0:r0 + MLA_NOPE] * scale).astype(BF16)
        qt_ref[0, r0 + MLA_NOPE:r0 + MLA_NOPE + half, :] = (
            (x1 * cos_t - x2 * sin_t) * scale).astype(BF16)
        qt_ref[0, r0 + MLA_NOPE + half:r0 + MLA_QK, :] = (
            (x1 * sin_t + x2 * cos_t) * scale).astype(BF16)

    t = k_pair * cs_ref[...]
    k_rope = (t + pltpu.roll(t, MLA_ROPE, axis=1))[:, :MLA_ROPE].astype(BF16)
    k_nope = _dot(ckvn, wkn_ref[...])
    for h in range(MLA_HEADS):
        kc_ref[0, h, :, :MLA_NOPE] = k_nope[:, h * MLA_NOPE:(h + 1) * MLA_NOPE].astype(BF16)
        kc_ref[0, h, :, MLA_NOPE:] = k_rope
    vt_ref[0] = _dot_nt(wvt_ref[...], ckvn).astype(BF16)


def _in_proj_call(x, g1, wh, wc, gq, gkv, wqt, wkn, wvt, cs, cos_t, sin_t, *, tm, scale):
    bsz, seq, d = x.shape
    grid = (bsz, seq // tm)
    full = lambda a: pl.BlockSpec(a.shape, lambda b, i: (0,) * a.ndim, pipeline_mode=pl.Buffered(1))
    tok = lambda w: pl.BlockSpec((1, tm, w), lambda b, i: (b, i, 0))
    in_specs = [
        tok(d), full(g1), full(wh), full(wc), full(gq), full(gkv), full(wqt), full(wkn), full(wvt),
        pl.BlockSpec((tm, 2 * MLA_ROPE), lambda b, i: (i, 0)),
        pl.BlockSpec((MLA_ROPE // 2, tm), lambda b, i: (0, i)),
        pl.BlockSpec((MLA_ROPE // 2, tm), lambda b, i: (0, i)),
    ]
    hshape = jax.ShapeDtypeStruct((bsz, seq, HGRN_WIDTH), F32)
    out_shape = [hshape] * 5 + [
        jax.ShapeDtypeStruct((bsz, MLA_HEADS * MLA_QK, seq), BF16),
        jax.ShapeDtypeStruct((bsz, MLA_HEADS, seq, MLA_QK), BF16),
        jax.ShapeDtypeStruct((bsz, MLA_WIDTH, seq), BF16),
    ]
    out_specs = [tok(HGRN_WIDTH)] * 5 + [
        pl.BlockSpec((1, MLA_HEADS * MLA_QK, tm), lambda b, i: (b, 0, i)),
        pl.BlockSpec((1, MLA_HEADS, tm, MLA_QK), lambda b, i: (b, 0, i, 0)),
        pl.BlockSpec((1, MLA_WIDTH, tm), lambda b, i: (b, 0, i)),
    ]
    return pl.pallas_call(
        functools.partial(_in_proj_kernel, scale=scale),
        grid=grid, in_specs=in_specs, out_specs=out_specs, out_shape=out_shape,
        compiler_params=pltpu.CompilerParams(
            dimension_semantics=("parallel", "parallel"), vmem_limit_bytes=VMEM_LIMIT),
        name="in_proj",
    )(x, g1, wh, wc, gq, gkv, wqt, wkn, wvt, cs, cos_t, sin_t)


def _hgrn_kernel(hq_ref, hi_ref, hf_ref, hb_ref, hg_ref, lbl_ref, gn_ref, y_ref, oacc_ref, *,
                 n_groups, group):
    c = HGRN_BLOCK
    rg = group * c
    row = lax.broadcasted_iota(jnp.int32, (TRI_ROWS, TRI_ROWS), 0)
    col = lax.broadcasted_iota(jnp.int32, (TRI_ROWS, TRI_ROWS), 1)
    same_block = (row // c) == (col // c)
    tri_fwd = (same_block & (col <= row)).astype(BF16)
    tri_bwd = (same_block & (col >= row)).astype(BF16)
    lane = lax.broadcasted_iota(jnp.int32, (c, PAIR), 1)
    head0 = lane < HGRN_DK
    slane = lax.broadcasted_iota(jnp.int32, (c, 2 * c), 1)
    srow = lax.broadcasted_iota(jnp.int32, (c, 2 * c), 0)
    key_pos = jnp.where(slane < c, slane, slane - c)
    brow = lax.broadcasted_iota(jnp.int32, (PAIR, PAIR), 0)
    bcol = lax.broadcasted_iota(jnp.int32, (PAIR, PAIR), 1)
    same_head = (brow < HGRN_DK) == (bcol < HGRN_DK)
    head_mean = jnp.where(same_head, 1.0 / HGRN_DK, 0.0).astype(BF16)

    lbl = lbl_ref[...]

    def lower_bound(d):
        l0 = lbl[2 * d:2 * d + 1]
        l1 = lbl[2 * d + 1:2 * d + 2]
        m = jnp.maximum(l0, l1)
        e0 = jnp.exp(l0 - m)
        e1 = jnp.exp(l1 - m)
        return e0 / (e0 + e1)

    def sweep_group(n, st, gate_ref, lb, reverse):
        rows = pl.ds(pl.multiple_of(n * rg, rg), rg)
        hq2 = 0.5 * hq_ref[0, rows, :]
        q = hq2 + hq2 * jnp.tanh(hq2)
        v16 = hi_ref[0, rows, :].astype(BF16)
        b = 0.5 * (1.0 - lb)
        bt = b * jnp.tanh(0.5 * gate_ref[0, rows, :])
        log2_f = jnp.log2((0.5 * (1.0 + lb)) + bt)
        k = b - bt
        tri = tri_bwd if reverse else tri_fwd
        keep = (key_pos >= srow) if reverse else (key_pos <= srow)
        lf_hi, lf_lo = _split_bf16(log2_f)
        cum = jnp.concatenate(
            [_dot(tri, lf_hi[r:r + TRI_ROWS]) + _dot(tri, lf_lo[r:r + TRI_ROWS])
             for r in range(0, rg, TRI_ROWS)], axis=0)
        zero16 = jnp.zeros((c, PAIR), BF16)
        order = range(group - 1, -1, -1) if reverse else range(group)
        sl = [slice(g * c, (g + 1) * c) for g in range(group)]
        mid = c // 2 if reverse else c // 2 - 1
        q_dec, decay, scores, u_t = {}, {}, {}, {}
        for g in order:
            cg = cum[sl[g]]
            cm = cg[mid:mid + 1]
            total = cg[0:1] if reverse else cg[c - 1:c]
            q_mid = q[sl[g]] * jnp.exp2(cg - cm)
            k_mid = k[sl[g]] * jnp.exp2(cm - cg)
            q_dec[g] = (q_mid * jnp.exp2(cm)).astype(BF16)
            k_end = (k_mid * jnp.exp2(total - cm)).astype(BF16)
            decay[g] = jnp.exp2(total)
            ki = k_mid.astype(BF16)
            kk = jnp.concatenate([jnp.where(head0, ki, zero16), jnp.where(head0, zero16, ki)], axis=0)
            scores[g] = _dot_nt(q_mid.astype(BF16), kk)
            u_t[g] = _dot_tn(v16[sl[g]], k_end)
        sts = {}
        for g in order:
            sts[g] = st.T.astype(BF16)
            st = st * decay[g] + jnp.where(same_head, u_t[g], 0.0)
        outs = [None] * group
        for g in order:
            vg = v16[sl[g]]
            vv = jnp.concatenate([jnp.where(head0, vg, zero16), jnp.where(head0, zero16, vg)], axis=0)
            p = jnp.where(keep, scores[g], 0.0).astype(BF16)
            outs[g] = _dot(p, vv) + _dot(q_dec[g], sts[g])
        return rows, jnp.concatenate(outs, axis=0), st

    st0 = jnp.zeros((PAIR, PAIR), F32)

    lb_f = lower_bound(0)

    def fwd_body(n, st):
        rows, o, st = sweep_group(n, st, hf_ref, lb_f, False)
        oacc_ref[rows, :] = o
        return st

    lax.fori_loop(0, n_groups, fwd_body, st0)

    lb_b = lower_bound(1)
    gn = gn_ref[...]

    def bwd_body(i, st):
        rows, o, st = sweep_group(n_groups - 1 - i, st, hb_ref, lb_b, True)
        o = o + oacc_ref[rows, :]
        ms = _dot((o * o).astype(BF16), head_mean)
        g2 = 0.5 * hg_ref[0, rows, :]
        gate = g2 + g2 * jnp.tanh(g2)
        y_ref[0, rows, :] = (o * lax.rsqrt(ms + EPS) * gn * gate).astype(y_ref.dtype)
        return st

    lax.fori_loop(0, n_groups, bwd_body, st0)


def _hgrn_call(hq, hi, hf, hb, hg, lbl, gn, *, group):
    bsz, seq, _ = hq.shape
    grid = (bsz, HGRN_WIDTH // PAIR)
    blk = pl.BlockSpec((1, seq, PAIR), lambda b, p: (b, 0, p))
    return pl.pallas_call(
        functools.partial(_hgrn_kernel, n_groups=seq // (HGRN_BLOCK * group), group=group),
        grid=grid,
        in_specs=[blk, blk, blk, blk, blk,
                  pl.BlockSpec((4, PAIR), lambda b, p: (0, p)),
                  pl.BlockSpec((1, PAIR), lambda b, p: (0, p))],
        out_specs=blk,
        out_shape=jax.ShapeDtypeStruct((bsz, seq, HGRN_WIDTH), BF16),
        scratch_shapes=[pltpu.VMEM((seq, PAIR), F32)],
        compiler_params=pltpu.CompilerParams(
            dimension_semantics=("parallel", "parallel"), vmem_limit_bytes=VMEM_LIMIT),
        name="hgrn2",
    )(hq, hi, hf, hb, hg, lbl, gn)


def _attn_kernel(qt_ref, kc_ref, vt_ref, o_ref, s0_ref, s1_ref, m0_ref, m1_ref, acc_ref, *, tk, tq):
    r = pl.program_id(0)
    seq = kc_ref.shape[2]
    n_qblocks = qt_ref.shape[2] // tq

    @pl.when(r == 0)
    def _():
        s1_ref[...] = jnp.zeros_like(s1_ref)
        m1_ref[...] = jnp.zeros_like(m1_ref)

    def step(s_cur, s_prev, m_cur_ref, m_prev_ref):
        for j in range(n_qblocks):
            qs = slice(j * tq, (j + 1) * tq)
            q_t = qt_ref[0, :, qs]
            m_prev = m_prev_ref[:, qs]
            m_cur = None
            l = None
            for c in range(seq // tk):
                ks = slice(c * tk, (c + 1) * tk)
                s = _dot(kc_ref[0, 0, ks, :], q_t)
                cm = jnp.max(s, axis=0, keepdims=True)
                m_cur = cm if m_cur is None else jnp.maximum(m_cur, cm)
                p = jnp.exp2(s_prev[ks, qs] - m_prev)
                s_cur[ks, qs] = s
                ps = jnp.sum(p, axis=0, keepdims=True)
                l = ps if l is None else l + ps
                pv = _dot(vt_ref[0, :, ks], p.astype(BF16))
                if c == 0:
                    acc_ref[...] = pv
                else:
                    acc_ref[...] += pv
            m_cur_ref[:, qs] = m_cur
            o_ref[0, qs, :] = (acc_ref[...] / l).T

    @pl.when(r % 2 == 0)
    def _():
        step(s0_ref, s1_ref, m0_ref, m1_ref)

    @pl.when(r % 2 == 1)
    def _():
        step(s1_ref, s0_ref, m1_ref, m0_ref)


def _attn_call(qt, kc, vt, *, tq, tk, n_qblocks):
    bsz, _, seq = qt.shape
    tqs = tq * n_qblocks
    nq = seq // tqs
    n_prob = bsz * MLA_HEADS * nq

    def prob(r):
        r = jnp.minimum(r, n_prob - 1)
        return r // (MLA_HEADS * nq), (r // nq) % MLA_HEADS, r % nq

    def lag(r):
        return prob(jnp.maximum(r - 1, 0))

    return pl.pallas_call(
        functools.partial(_attn_kernel, tk=tk, tq=tq),
        grid=(n_prob + 1,),
        in_specs=[
            pl.BlockSpec((1, MLA_QK, tqs), lambda r: prob(r)),
            pl.BlockSpec((1, 1, seq, MLA_QK), lambda r: prob(r)[:2] + (0, 0)),
            pl.BlockSpec((1, MLA_V, seq), lambda r: lag(r)[:2] + (0,)),
        ],
        out_specs=pl.BlockSpec((1, tqs, MLA_V), lambda r: (lag(r)[0], lag(r)[2], lag(r)[1])),
        out_shape=jax.ShapeDtypeStruct((bsz, seq, MLA_WIDTH), F32),
        scratch_shapes=[pltpu.VMEM((seq, tqs), F32), pltpu.VMEM((seq, tqs), F32),
                        pltpu.VMEM((1, tqs), F32), pltpu.VMEM((1, tqs), F32),
                        pltpu.VMEM((MLA_V, tq), F32)],
        compiler_params=pltpu.CompilerParams(
            dimension_semantics=("arbitrary",), vmem_limit_bytes=VMEM_LIMIT),
        name="mla_attn",
    )(qt, kc, vt)


def _out_ffn_kernel(x_ref, ya_ref, ob_ref, gm_ref, wo_ref, g2_ref, wg_ref, wu_ref, wd_ref, gf_ref,
                    out_ref, *, n_sub):
    tm = x_ref.shape[1]
    ts = tm // n_sub
    subs = [slice(i * ts, (i + 1) * ts) for i in range(n_sub)]
    x1 = [x_ref[0, r, :] + _dot(ya_ref[0, r, :], wo_ref[:HGRN_WIDTH, :])
          + _dot(_rms(ob_ref[0, r, :], gm_ref[...]).astype(BF16), wo_ref[HGRN_WIDTH:, :]) for r in subs]
    h2 = [_rms(v, g2_ref[...]).astype(BF16) for v in x1]
    gate = [_dot(h, wg_ref[...]) for h in h2]
    up = [_dot(h, wu_ref[...]) for h in h2]
    a = [(g * _sigmoid(g) * u).astype(BF16) for g, u in zip(gate, up)]
    for r, v, ai in zip(subs, x1, a):
        out_ref[0, r, :] = _rms(v + _dot(ai, wd_ref[...]), gf_ref[...])


def _out_ffn_call(x, ya, ob, gm, wo, g2, wg, wu, wd, gf, *, tm, n_sub):
    bsz, seq, d = x.shape
    grid = (bsz, seq // tm)
    tok = lambda w: pl.BlockSpec((1, tm, w), lambda b, i: (b, i, 0))
    full = lambda a: pl.BlockSpec(a.shape, lambda b, i: (0,) * a.ndim,
                                  pipeline_mode=pl.Buffered(1))
    return pl.pallas_call(
        functools.partial(_out_ffn_kernel, n_sub=n_sub),
        grid=grid,
        in_specs=[tok(d), tok(HGRN_WIDTH), tok(MLA_WIDTH), full(gm), full(wo), full(g2),
                  full(wg), full(wu), full(wd), full(gf)],
        out_specs=tok(d),
        out_shape=jax.ShapeDtypeStruct((bsz, seq, d), x.dtype),
        compiler_params=pltpu.CompilerParams(
            dimension_semantics=("parallel", "parallel"), vmem_limit_bytes=VMEM_LIMIT),
        name="out_ffn",
    )(x, ya, ob, gm, wo, g2, wg, wu, wd, gf)


class _Tiles(NamedTuple):
    proj_rows: int
    hgrn_group: int
    attn_q: int
    attn_qblocks: int
    attn_k: int
    ffn_rows: int
    ffn_sub: int


def _tiles(seq):
    rows = min(512, seq)
    return _Tiles(proj_rows=min(2 * rows, seq), hgrn_group=min(32, seq // HGRN_BLOCK),
                  attn_q=rows, attn_qblocks=min(2, seq // rows), attn_k=rows, ffn_rows=rows, ffn_sub=2)


def _rope_tables(seq):
    inv = 1.0 / (ROPE_THETA ** (jnp.arange(0, MLA_ROPE, 2, dtype=F32) / MLA_ROPE))
    ang = jnp.arange(seq, dtype=F32)[:, None] * inv[None, :]
    return jnp.cos(ang), jnp.sin(ang)


def kernel(x, norm1_g, w_in, lb_logits, hgrn_norm_g, q_a_norm_g, w_q_b, kv_a_norm_g, w_kv_b,
           mla_norm_g, w_out, norm2_g, w_gate, w_up, w_down, final_norm_g):
    bsz, seq, d = x.shape
    assert w_in.shape[0] == 1 and lb_logits.shape[1] == 2, "single-layer trunk only"
    t = _tiles(seq)
    n_h = 5 * HGRN_WIDTH

    w_in0 = w_in[0]
    wh = w_in0[:, :n_h].astype(BF16)
    w_kr = w_in0[:, n_h + Q_LORA + KV_LORA:]
    half = MLA_ROPE // 2
    w_kr_rot = jnp.concatenate([-w_kr[:, half:], w_kr[:, :half]], axis=1)
    wc = jnp.concatenate([w_in0[:, n_h:], w_kr_rot], axis=1).astype(BF16)
    wqt = w_q_b[0].T.astype(BF16)
    w_kv = w_kv_b[0].reshape(KV_LORA, MLA_HEADS, MLA_NOPE + MLA_V)
    wkn = w_kv[:, :, :MLA_NOPE].reshape(KV_LORA, MLA_HEADS * MLA_NOPE).astype(BF16)
    wvt = w_kv[:, :, MLA_NOPE:].reshape(KV_LORA, MLA_WIDTH).T.astype(BF16)
    cos, sin = _rope_tables(seq)
    cs = jnp.concatenate([cos, cos, sin, sin], axis=1)
    row = lambda a: a.reshape(1, -1)

    hq, hi, hf, hb, hg, qt, kc, vt = _in_proj_call(
        x, row(norm1_g[0]), wh, wc, row(q_a_norm_g[0]), row(kv_a_norm_g[0]), wqt, wkn, wvt,
        cs, cos.T, sin.T, tm=t.proj_rows, scale=float(MLA_QK) ** -0.5 * math.log2(math.e))

    y_a = _hgrn_call(hq, hi, hf, hb, hg, lb_logits.reshape(4, HGRN_WIDTH), row(hgrn_norm_g[0]),
                     group=t.hgrn_group)
    o_b = _attn_call(qt, kc, vt, tq=t.attn_q, tk=t.attn_k, n_qblocks=t.attn_qblocks)

    return _out_ffn_call(
        x, y_a, o_b, row(mla_norm_g[0]), w_out[0].astype(BF16), row(norm2_g[0]),
        w_gate[0].astype(BF16), w_up[0].astype(BF16), w_down[0].astype(BF16), row(final_norm_g),
        tm=t.ffn_rows, n_sub=t.ffn_sub)
```

```python
import functools
import math
from typing import NamedTuple

import jax
import jax.numpy as jnp
from jax import lax
from jax.experimental import pallas as pl
from jax.experimental.pallas import tpu as pltpu

EPS = 1e-6
ROPE_THETA = 10000.0
HGRN_HEADS = 8
HGRN_DK = 64
HGRN_WIDTH = HGRN_HEADS * HGRN_DK
HGRN_CHUNK = 64
HGRN_BLOCK = 2 * HGRN_CHUNK
PAIR = 2 * HGRN_DK
TRI_ROWS = 256
MLA_HEADS = 4
MLA_NOPE = 128
MLA_ROPE = 64
MLA_V = 128
MLA_QK = MLA_NOPE + MLA_ROPE
MLA_WIDTH = MLA_HEADS * MLA_V
Q_LORA = 384
KV_LORA = 256
V7X_VMEM_BYTES = 64 * 2**20
VMEM_LIMIT = V7X_VMEM_BYTES * 7 // 8

BF16 = jnp.bfloat16
F32 = jnp.float32

_NT = (((1,), (1,)), ((), ()))
_TN = (((0,), (0,)), ((), ()))


def _dot(a, b):
    return jnp.dot(a, b, preferred_element_type=F32)


def _dot_nt(a, b):
    return lax.dot_general(a, b, _NT, preferred_element_type=F32)


def _dot_tn(a, b):
    return lax.dot_general(a, b, _TN, preferred_element_type=F32)


def _rms(x, g):
    return x * lax.rsqrt(jnp.mean(x * x, axis=-1, keepdims=True) + EPS) * g


def _silu(x):
    h = 0.5 * x
    return h + h * jnp.tanh(h)


def _split_bf16(x):
    hi = x.astype(BF16)
    lo = (x - hi.astype(F32)).astype(BF16)
    return hi, lo


def _in_proj_kernel(x_ref, g1_ref, wh_ref, wc_ref, gq_ref, gkv_ref, wqt_ref, wkn_ref, wvt_ref,
                    cs_ref, cost_ref, sint_ref,
                    hq_ref, hi_ref, hf_ref, hb_ref, hg_ref, qt_ref, kc_ref, vt_ref, *, scale):
    hn = _rms(x_ref[0], g1_ref[...]).astype(BF16)
    for g, o_ref in enumerate((hq_ref, hi_ref, hf_ref, hb_ref, hg_ref)):
        w = wh_ref[:, g * HGRN_WIDTH:(g + 1) * HGRN_WIDTH]
        o_ref[0] = _dot(hn, w).astype(o_ref.dtype)

    c = _dot(hn, wc_ref[...])
    c_q = c[:, :Q_LORA]
    c_kv = c[:, Q_LORA:Q_LORA + KV_LORA]
    kr0 = Q_LORA + KV_LORA
    k_pair = c[:, kr0:kr0 + 2 * MLA_ROPE]
    cqn = _rms(c_q, gq_ref[...]).astype(BF16)
    ckvn = _rms(c_kv, gkv_ref[...]).astype(BF16)

    qt = _dot_nt(wqt_ref[...], cqn)
    cos_t = cost_ref[...]
    sin_t = sint_ref[...]
    half = MLA_ROPE // 2
    for h in range(MLA_HEADS):
        r0 = h * MLA_QK
        x1 = qt[r0 + MLA_NOPE:r0 + MLA_NOPE + half]
        x2 = qt[r0 + MLA_NOPE + half:r0 + MLA_QK]
        qt_ref[0, r0:r0 + MLA_NOPE, :] = (qt[r0:r0 + MLA_NOPE] * scale).astype(BF16)
        qt_ref[0, r0 + MLA_NOPE:r0 + MLA_NOPE + half, :] = (
            (x1 * cos_t - x2 * sin_t) * scale).astype(BF16)
        qt_ref[0, r0 + MLA_NOPE + half:r0 + MLA_QK, :] = (
            (x1 * sin_t + x2 * cos_t) * scale).astype(BF16)

    t = k_pair * cs_ref[...]
    k_rope = (t + pltpu.roll(t, MLA_ROPE, axis=1))[:, :MLA_ROPE].astype(BF16)
    k_nope = _dot(ckvn, wkn_ref[...])
    for h in range(MLA_HEADS):
        kc_ref[0, h, :, :MLA_NOPE] = k_nope[:, h * MLA_NOPE:(h + 1) * MLA_NOPE].astype(BF16)
        kc_ref[0, h, :, MLA_NOPE:] = k_rope
    vt_ref[0] = _dot_nt(wvt_ref[...], ckvn).astype(BF16)


def _in_proj_call(x, g1, wh, wc, gq, gkv, wqt, wkn, wvt, cs, cos_t, sin_t, *, tm, scale):
    bsz, seq, d = x.shape
    grid = (bsz, seq // tm)
    full = lambda a: pl.BlockSpec(a.shape, lambda b, i: (0,) * a.ndim, pipeline_mode=pl.Buffered(1))
    tok = lambda w: pl.BlockSpec((1, tm, w), lambda b, i: (b, i, 0))
    in_specs = [
        tok(d), full(g1), full(wh), full(wc), full(gq), full(gkv), full(wqt), full(wkn), full(wvt),
        pl.BlockSpec((tm, 2 * MLA_ROPE), lambda b, i: (i, 0)),
        pl.BlockSpec((MLA_ROPE // 2, tm), lambda b, i: (0, i)),
        pl.BlockSpec((MLA_ROPE // 2, tm), lambda b, i: (0, i)),
    ]
    hshape = jax.ShapeDtypeStruct((bsz, seq, HGRN_WIDTH), F32)
    out_shape = [hshape] * 5 + [
        jax.ShapeDtypeStruct((bsz, MLA_HEADS * MLA_QK, seq), BF16),
        jax.ShapeDtypeStruct((bsz, MLA_HEADS, seq, MLA_QK), BF16),
        jax.ShapeDtypeStruct((bsz, MLA_WIDTH, seq), BF16),
    ]
    out_specs = [tok(HGRN_WIDTH)] * 5 + [
        pl.BlockSpec((1, MLA_HEADS * MLA_QK, tm), lambda b, i: (b, 0, i)),
        pl.BlockSpec((1, MLA_HEADS, tm, MLA_QK), lambda b, i: (b, 0, i, 0)),
        pl.BlockSpec((1, MLA_WIDTH, tm), lambda b, i: (b, 0, i)),
    ]
    return pl.pallas_call(
        functools.partial(_in_proj_kernel, scale=scale),
        grid=grid, in_specs=in_specs, out_specs=out_specs, out_shape=out_shape,
        compiler_params=pltpu.CompilerParams(
            dimension_semantics=("parallel", "parallel"), vmem_limit_bytes=VMEM_LIMIT),
        name="in_proj",
    )(x, g1, wh, wc, gq, gkv, wqt, wkn, wvt, cs, cos_t, sin_t)


def _hgrn_kernel(hq_ref, hi_ref, hf_ref, hb_ref, hg_ref, lbl_ref, gn_ref, y_ref, oacc_ref, *,
                 n_groups, group):
    c = HGRN_BLOCK
    rg = group * c
    row = lax.broadcasted_iota(jnp.int32, (TRI_ROWS, TRI_ROWS), 0)
    col = lax.broadcasted_iota(jnp.int32, (TRI_ROWS, TRI_ROWS), 1)
    same_block = (row // c) == (col // c)
    tri_fwd = (same_block & (col <= row)).astype(BF16)
    tri_bwd = (same_block & (col >= row)).astype(BF16)
    lane = lax.broadcasted_iota(jnp.int32, (c, PAIR), 1)
    head0 = lane < HGRN_DK
    slane = lax.broadcasted_iota(jnp.int32, (c, 2 * c), 1)
    srow = lax.broadcasted_iota(jnp.int32, (c, 2 * c), 0)
    key_pos = jnp.where(slane < c, slane, slane - c)
    brow = lax.broadcasted_iota(jnp.int32, (PAIR, PAIR), 0)
    bcol = lax.broadcasted_iota(jnp.int32, (PAIR, PAIR), 1)
    same_head = (brow < HGRN_DK) == (bcol < HGRN_DK)
    head_mean = jnp.where(same_head, 1.0 / HGRN_DK, 0.0).astype(BF16)

    lbl = lbl_ref[...]

    def lower_bound(d):
        l0 = lbl[2 * d:2 * d + 1]
        l1 = lbl[2 * d + 1:2 * d + 2]
        m = jnp.maximum(l0, l1)
        e0 = jnp.exp(l0 - m)
        e1 = jnp.exp(l1 - m)
        return e0 / (e0 + e1)

    def sweep_group(n, st, gate_ref, lb, reverse):
        rows = pl.ds(pl.multiple_of(n * rg, rg), rg)
        q = _silu(hq_ref[0, rows, :])
        v16 = hi_ref[0, rows, :].astype(BF16)
        b = 0.5 * (1.0 - lb)
        bt = b * jnp.tanh(0.5 * gate_ref[0, rows, :])
        log2_f = jnp.log2((0.5 * (1.0 + lb)) + bt)
        k = b - bt
        tri = tri_bwd if reverse else tri_fwd
        keep = (key_pos >= srow) if reverse else (key_pos <= srow)
        lf_hi, lf_lo = _split_bf16(log2_f)
        cum = jnp.concatenate(
            [_dot(tri, lf_hi[r:r + TRI_ROWS]) + _dot(tri, lf_lo[r:r + TRI_ROWS])
             for r in range(0, rg, TRI_ROWS)], axis=0)
        zero16 = jnp.zeros((c, PAIR), BF16)
        order = range(group - 1, -1, -1) if reverse else range(group)
        sl = [slice(g * c, (g + 1) * c) for g in range(group)]
        mid = c // 2 if reverse else c // 2 - 1
        q_dec, decay, scores, u_t = {}, {}, {}, {}
        for g in order:
            cg = cum[sl[g]]
            cm = cg[mid:mid + 1]
            total = cg[0:1] if reverse else cg[c - 1:c]
            q_mid = q[sl[g]] * jnp.exp2(cg - cm)
            k_mid = k[sl[g]] * jnp.exp2(cm - cg)
            q_dec[g] = (q_mid * jnp.exp2(cm)).astype(BF16)
            k_end = (k_mid * jnp.exp2(total - cm)).astype(BF16)
            decay[g] = jnp.exp2(total)
            ki = k_mid.astype(BF16)
            kk = jnp.concatenate([jnp.where(head0, ki, zero16), jnp.where(head0, zero16, ki)], axis=0)
            scores[g] = _dot_nt(q_mid.astype(BF16), kk)
            u_t[g] = _dot_tn(v16[sl[g]], k_end)
        sts = {}
        for g in order:
            sts[g] = st.T.astype(BF16)
            st = st * decay[g] + jnp.where(same_head, u_t[g], 0.0)
        outs = [None] * group
        for g in order:
            vg = v16[sl[g]]
            vv = jnp.concatenate([jnp.where(head0, vg, zero16), jnp.where(head0, zero16, vg)], axis=0)
            p = jnp.where(keep, scores[g], 0.0).astype(BF16)
            outs[g] = _dot(p, vv) + _dot(q_dec[g], sts[g])
        return rows, jnp.concatenate(outs, axis=0), st

    st0 = jnp.zeros((PAIR, PAIR), F32)

    lb_f = lower_bound(0)

    def fwd_body(n, st):
        rows, o, st = sweep_group(n, st, hf_ref, lb_f, False)
        oacc_ref[rows, :] = o
        return st

    lax.fori_loop(0, n_groups, fwd_body, st0)

    lb_b = lower_bound(1)
    gn = gn_ref[...]

    def bwd_body(i, st):
        rows, o, st = sweep_group(n_groups - 1 - i, st, hb_ref, lb_b, True)
        o = o + oacc_ref[rows, :]
        ms = _dot((o * o).astype(BF16), head_mean)
        gate = _silu(hg_ref[0, rows, :])
        y_ref[0, rows, :] = (o * lax.rsqrt(ms + EPS) * gn * gate).astype(y_ref.dtype)
        return st

    lax.fori_loop(0, n_groups, bwd_body, st0)


def _hgrn_call(hq, hi, hf, hb, hg, lbl, gn, *, group):
    bsz, seq, _ = hq.shape
    grid = (bsz, HGRN_WIDTH // PAIR)
    blk = pl.BlockSpec((1, seq, PAIR), lambda b, p: (b, 0, p))
    return pl.pallas_call(
        functools.partial(_hgrn_kernel, n_groups=seq // (HGRN_BLOCK * group), group=group),
        grid=grid,
        in_specs=[blk, blk, blk, blk, blk,
                  pl.BlockSpec((4, PAIR), lambda b, p: (0, p)),
                  pl.BlockSpec((1, PAIR), lambda b, p: (0, p))],
        out_specs=blk,
        out_shape=jax.ShapeDtypeStruct((bsz, seq, HGRN_WIDTH), BF16),
        scratch_shapes=[pltpu.VMEM((seq, PAIR), F32)],
        compiler_params=pltpu.CompilerParams(
            dimension_semantics=("parallel", "parallel"), vmem_limit_bytes=VMEM_LIMIT),
        name="hgrn2",
    )(hq, hi, hf, hb, hg, lbl, gn)


def _attn_kernel(qt_ref, kc_ref, vt_ref, o_ref, s0_ref, s1_ref, m0_ref, m1_ref, acc_ref, *, tk, tq):
    r = pl.program_id(0)
    seq = kc_ref.shape[2]
    n_qblocks = qt_ref.shape[2] // tq

    @pl.when(r == 0)
    def _():
        s1_ref[...] = jnp.zeros_like(s1_ref)
        m1_ref[...] = jnp.zeros_like(m1_ref)

    def step(s_cur, s_prev, m_cur_ref, m_prev_ref):
        for j in range(n_qblocks):
            qs = slice(j * tq, (j + 1) * tq)
            q_t = qt_ref[0, :, qs]
            m_prev = m_prev_ref[:, qs]
            m_cur = None
            l = None
            for c in range(seq // tk):
                ks = slice(c * tk, (c + 1) * tk)
                s = _dot(kc_ref[0, 0, ks, :], q_t)
                cm = jnp.max(s, axis=0, keepdims=True)
                m_cur = cm if m_cur is None else jnp.maximum(m_cur, cm)
                p = jnp.exp2(s_prev[ks, qs] - m_prev)
                s_cur[ks, qs] = s
                ps = jnp.sum(p, axis=0, keepdims=True)
                l = ps if l is None else l + ps
                pv = _dot(vt_ref[0, :, ks], p.astype(BF16))
                if c == 0:
                    acc_ref[...] = pv
                else:
                    acc_ref[...] += pv
            m_cur_ref[:, qs] = m_cur
            o_ref[0, qs, :] = (acc_ref[...] / l).T

    @pl.when(r % 2 == 0)
    def _():
        step(s0_ref, s1_ref, m0_ref, m1_ref)

    @pl.when(r % 2 == 1)
    def _():
        step(s1_ref, s0_ref, m1_ref, m0_ref)


def _attn_call(qt, kc, vt, *, tq, tk, n_qblocks):
    bsz, _, seq = qt.shape
    tqs = tq * n_qblocks
    nq = seq // tqs
    n_prob = bsz * MLA_HEADS * nq

    def prob(r):
        r = jnp.minimum(r, n_prob - 1)
        return r // (MLA_HEADS * nq), (r // nq) % MLA_HEADS, r % nq

    def lag(r):
        return prob(jnp.maximum(r - 1, 0))

    return pl.pallas_call(
        functools.partial(_attn_kernel, tk=tk, tq=tq),
        grid=(n_prob + 1,),
        in_specs=[
            pl.BlockSpec((1, MLA_QK, tqs), lambda r: prob(r)),
            pl.BlockSpec((1, 1, seq, MLA_QK), lambda r: prob(r)[:2] + (0, 0)),
            pl.BlockSpec((1, MLA_V, seq), lambda r: lag(r)[:2] + (0,)),
        ],
        out_specs=pl.BlockSpec((1, tqs, MLA_V), lambda r: (lag(r)[0], lag(r)[2], lag(r)[1])),
        out_shape=jax.ShapeDtypeStruct((bsz, seq, MLA_WIDTH), F32),
        scratch_shapes=[pltpu.VMEM((seq, tqs), F32), pltpu.VMEM((seq, tqs), F32),
                        pltpu.VMEM((1, tqs), F32), pltpu.VMEM((1, tqs), F32),
                        pltpu.VMEM((MLA_V, tq), F32)],
        compiler_params=pltpu.CompilerParams(
            dimension_semantics=("arbitrary",), vmem_limit_bytes=VMEM_LIMIT),
        name="mla_attn",
    )(qt, kc, vt)


def _out_ffn_kernel(x_ref, ya_ref, ob_ref, gm_ref, wo_ref, g2_ref, wg_ref, wu_ref, wd_ref, gf_ref,
                    out_ref, *, n_sub):
    tm = x_ref.shape[1]
    ts = tm // n_sub
    subs = [slice(i * ts, (i + 1) * ts) for i in range(n_sub)]
    x1 = [x_ref[0, r, :] + _dot(ya_ref[0, r, :], wo_ref[:HGRN_WIDTH, :])
          + _dot(_rms(ob_ref[0, r, :], gm_ref[...]).astype(BF16), wo_ref[HGRN_WIDTH:, :]) for r in subs]
    h2 = [_rms(v, g2_ref[...]).astype(BF16) for v in x1]
    gate = [_dot(h, wg_ref[...]) for h in h2]
    up = [_dot(h, wu_ref[...]) for h in h2]
    a = [(_silu(g) * u).astype(BF16) for g, u in zip(gate, up)]
    for r, v, ai in zip(subs, x1, a):
        out_ref[0, r, :] = _rms(v + _dot(ai, wd_ref[...]), gf_ref[...])


def _out_ffn_call(x, ya, ob, gm, wo, g2, wg, wu, wd, gf, *, tm, n_sub):
    bsz, seq, d = x.shape
    grid = (bsz, seq // tm)
    tok = lambda w: pl.BlockSpec((1, tm, w), lambda b, i: (b, i, 0))
    full = lambda a: pl.BlockSpec(a.shape, lambda b, i: (0,) * a.ndim,
                                  pipeline_mode=pl.Buffered(1))
    return pl.pallas_call(
        functools.partial(_out_ffn_kernel, n_sub=n_sub),
        grid=grid,
        in_specs=[tok(d), tok(HGRN_WIDTH), tok(MLA_WIDTH), full(gm), full(wo), full(g2),
                  full(wg), full(wu), full(wd), full(gf)],
        out_specs=tok(d),
        out_shape=jax.ShapeDtypeStruct((bsz, seq, d), x.dtype),
        compiler_params=pltpu.CompilerParams(
            dimension_semantics=("parallel", "parallel"), vmem_limit_bytes=VMEM_LIMIT),
        name="out_ffn",
    )(x, ya, ob, gm, wo, g2, wg, wu, wd, gf)


class _Tiles(NamedTuple):
    proj_rows: int
    hgrn_group: int
    attn_q: int
    attn_qblocks: int
    attn_k: int
    ffn_rows: int
    ffn_sub: int


def _tiles(seq):
    rows = min(512, seq)
    return _Tiles(proj_rows=min(2 * rows, seq), hgrn_group=min(32, seq // HGRN_BLOCK),
                  attn_q=rows, attn_qblocks=min(2, seq // rows), attn_k=rows, ffn_rows=rows, ffn_sub=2)


def _rope_tables(seq):
    inv = 1.0 / (ROPE_THETA ** (jnp.arange(0, MLA_ROPE, 2, dtype=F32) / MLA_ROPE))
    ang = jnp.arange(seq, dtype=F32)[:, None] * inv[None, :]
    return jnp.cos(ang), jnp.sin(ang)


def kernel(x, norm1_g, w_in, lb_logits, hgrn_norm_g, q_a_norm_g, w_q_b, kv_a_norm_g, w_kv_b,
           mla_norm_g, w_out, norm2_g, w_gate, w_up, w_down, final_norm_g):
    bsz, seq, d = x.shape
    assert w_in.shape[0] == 1 and lb_logits.shape[1] == 2, "single-layer trunk only"
    t = _tiles(seq)
    n_h = 5 * HGRN_WIDTH

    w_in0 = w_in[0]
    wh = w_in0[:, :n_h].astype(BF16)
    w_kr = w_in0[:, n_h + Q_LORA + KV_LORA:]
    half = MLA_ROPE // 2
    w_kr_rot = jnp.concatenate([-w_kr[:, half:], w_kr[:, :half]], axis=1)
    wc = jnp.concatenate([w_in0[:, n_h:], w_kr_rot], axis=1).astype(BF16)
    wqt = w_q_b[0].T.astype(BF16)
    w_kv = w_kv_b[0].reshape(KV_LORA, MLA_HEADS, MLA_NOPE + MLA_V)
    wkn = w_kv[:, :, :MLA_NOPE].reshape(KV_LORA, MLA_HEADS * MLA_NOPE).astype(BF16)
    wvt = w_kv[:, :, MLA_NOPE:].reshape(KV_LORA, MLA_WIDTH).T.astype(BF16)
    cos, sin = _rope_tables(seq)
    cs = jnp.concatenate([cos, cos, sin, sin], axis=1)
    row = lambda a: a.reshape(1, -1)

    hq, hi, hf, hb, hg, qt, kc, vt = _in_proj_call(
        x, row(norm1_g[0]), wh, wc, row(q_a_norm_g[0]), row(kv_a_norm_g[0]), wqt, wkn, wvt,
        cs, cos.T, sin.T, tm=t.proj_rows, scale=float(MLA_QK) ** -0.5 * math.log2(math.e))

    y_a = _hgrn_call(hq, hi, hf, hb, hg, lb_logits.reshape(4, HGRN_WIDTH), row(hgrn_norm_g[0]),
                     group=t.hgrn_group)
    o_b = _attn_call(qt, kc, vt, tq=t.attn_q, tk=t.attn_k, n_qblocks=t.attn_qblocks)

    return _out_ffn_call(
        x, y_a, o_b, row(mla_norm_g[0]), w_out[0].astype(BF16), row(norm2_g[0]),
        w_gate[0].astype(BF16), w_up[0].astype(BF16), w_down[0].astype(BF16), row(final_norm_g),
        tm=t.ffn_rows, n_sub=t.ffn_sub)
```

```python
import functools
import math
from typing import NamedTuple

import jax
import jax.numpy as jnp
from jax import lax
from jax.experimental import pallas as pl
from jax.experimental.pallas import tpu as pltpu

EPS = 1e-6
ROPE_THETA = 10000.0
HGRN_HEADS = 8
HGRN_DK = 64
HGRN_WIDTH = HGRN_HEADS * HGRN_DK
HGRN_CHUNK = 64
HGRN_BLOCK = 2 * HGRN_CHUNK
PAIR = 2 * HGRN_DK
TRI_ROWS = 256
MLA_HEADS = 4
MLA_NOPE = 128
MLA_ROPE = 64
MLA_V = 128
MLA_QK = MLA_NOPE + MLA_ROPE
MLA_WIDTH = MLA_HEADS * MLA_V
Q_LORA = 384
KV_LORA = 256
V7X_VMEM_BYTES = 64 * 2**20
VMEM_LIMIT = V7X_VMEM_BYTES * 7 // 8

BF16 = jnp.bfloat16
F32 = jnp.float32

_NT = (((1,), (1,)), ((), ()))
_TN = (((0,), (0,)), ((), ()))


def _dot(a, b):
    return jnp.dot(a, b, preferred_element_type=F32)


def _dot_nt(a, b):
    return lax.dot_general(a, b, _NT, preferred_element_type=F32)


def _dot_tn(a, b):
    return lax.dot_general(a, b, _TN, preferred_element_type=F32)


def _rms(x, g):
    return x * lax.rsqrt(jnp.mean(x * x, axis=-1, keepdims=True) + EPS) * g


def _silu(x):
    h = 0.5 * x
    return h + h * jnp.tanh(h)


def _split_bf16(x):
    hi = x.astype(BF16)
    lo = (x - hi.astype(F32)).astype(BF16)
    return hi, lo


def _in_proj_kernel(x_ref, g1_ref, wh_ref, wc_ref, gq_ref, gkv_ref, wqt_ref, wkn_ref, wvt_ref,
                    cs_ref, cost_ref, sint_ref,
                    hq_ref, hi_ref, hf_ref, hb_ref, hg_ref, qt_ref, kc_ref, vt_ref, *, scale):
    hn = _rms(x_ref[0], g1_ref[...]).astype(BF16)
    for g, o_ref in enumerate((hq_ref, hi_ref, hf_ref, hb_ref, hg_ref)):
        r = _dot(hn, wh_ref[:, g * HGRN_WIDTH:(g + 1) * HGRN_WIDTH])
        for p in range(HGRN_WIDTH // PAIR):
            o_ref[0, p] = r[:, p * PAIR:(p + 1) * PAIR].astype(o_ref.dtype)

    c = _dot(hn, wc_ref[...])
    c_q = c[:, :Q_LORA]
    c_kv = c[:, Q_LORA:Q_LORA + KV_LORA]
    kr0 = Q_LORA + KV_LORA
    k_pair = c[:, kr0:kr0 + 2 * MLA_ROPE]
    cqn = _rms(c_q, gq_ref[...]).astype(BF16)
    ckvn = _rms(c_kv, gkv_ref[...]).astype(BF16)

    qt = _dot_nt(wqt_ref[...], cqn)
    cos_t = cost_ref[...]
    sin_t = sint_ref[...]
    half = MLA_ROPE // 2
    for h in range(MLA_HEADS):
        r0 = h * MLA_QK
        x1 = qt[r0 + MLA_NOPE:r0 + MLA_NOPE + half]
        x2 = qt[r0 + MLA_NOPE + half:r0 + MLA_QK]
        qt_ref[0, r0:r0 + MLA_NOPE, :] = (qt[r0:r0 + MLA_NOPE] * scale).astype(BF16)
        qt_ref[0, r0 + MLA_NOPE:r0 + MLA_NOPE + half, :] = (
            (x1 * cos_t - x2 * sin_t) * scale).astype(BF16)
        qt_ref[0, r0 + MLA_NOPE + half:r0 + MLA_QK, :] = (
            (x1 * sin_t + x2 * cos_t) * scale).astype(BF16)

    t = k_pair * cs_ref[...]
    k_rope = (t + pltpu.roll(t, MLA_ROPE, axis=1))[:, :MLA_ROPE].astype(BF16)
    k_nope = _dot(ckvn, wkn_ref[...])
    for h in range(MLA_HEADS):
        kc_ref[0, h, :, :MLA_NOPE] = k_nope[:, h * MLA_NOPE:(h + 1) * MLA_NOPE].astype(BF16)
        kc_ref[0, h, :, MLA_NOPE:] = k_rope
    vt_ref[0] = _dot_nt(wvt_ref[...], ckvn).astype(BF16)


def _in_proj_call(x, g1, wh, wc, gq, gkv, wqt, wkn, wvt, cs, cos_t, sin_t, *, tm, scale):
    bsz, seq, d = x.shape
    grid = (bsz, seq // tm)
    full = lambda a: pl.BlockSpec(a.shape, lambda b, i: (0,) * a.ndim, pipeline_mode=pl.Buffered(1))
    tok = lambda w: pl.BlockSpec((1, tm, w), lambda b, i: (b, i, 0))
    in_specs = [
        tok(d), full(g1), full(wh), full(wc), full(gq), full(gkv), full(wqt), full(wkn), full(wvt),
        pl.BlockSpec((tm, 2 * MLA_ROPE), lambda b, i: (i, 0)),
        pl.BlockSpec((MLA_ROPE // 2, tm), lambda b, i: (0, i)),
        pl.BlockSpec((MLA_ROPE // 2, tm), lambda b, i: (0, i)),
    ]
    hshape = jax.ShapeDtypeStruct((bsz, HGRN_WIDTH // PAIR, seq, PAIR), F32)
    out_shape = [hshape] * 5 + [
        jax.ShapeDtypeStruct((bsz, MLA_HEADS * MLA_QK, seq), BF16),
        jax.ShapeDtypeStruct((bsz, MLA_HEADS, seq, MLA_QK), BF16),
        jax.ShapeDtypeStruct((bsz, MLA_WIDTH, seq), BF16),
    ]
    out_specs = [pl.BlockSpec((1, HGRN_WIDTH // PAIR, tm, PAIR), lambda b, i: (b, 0, i, 0))] * 5 + [
        pl.BlockSpec((1, MLA_HEADS * MLA_QK, tm), lambda b, i: (b, 0, i)),
        pl.BlockSpec((1, MLA_HEADS, tm, MLA_QK), lambda b, i: (b, 0, i, 0)),
        pl.BlockSpec((1, MLA_WIDTH, tm), lambda b, i: (b, 0, i)),
    ]
    return pl.pallas_call(
        functools.partial(_in_proj_kernel, scale=scale),
        grid=grid, in_specs=in_specs, out_specs=out_specs, out_shape=out_shape,
        compiler_params=pltpu.CompilerParams(
            dimension_semantics=("parallel", "parallel"), vmem_limit_bytes=VMEM_LIMIT),
        name="in_proj",
    )(x, g1, wh, wc, gq, gkv, wqt, wkn, wvt, cs, cos_t, sin_t)


def _hgrn_kernel(hq_ref, hi_ref, hf_ref, hb_ref, hg_ref, lbl_ref, gn_ref, y_ref, oacc_ref, *,
                 n_groups, group):
    c = HGRN_BLOCK
    rg = group * c
    row = lax.broadcasted_iota(jnp.int32, (TRI_ROWS, TRI_ROWS), 0)
    col = lax.broadcasted_iota(jnp.int32, (TRI_ROWS, TRI_ROWS), 1)
    same_block = (row // c) == (col // c)
    tri_fwd = (same_block & (col <= row)).astype(BF16)
    tri_bwd = (same_block & (col >= row)).astype(BF16)
    lane = lax.broadcasted_iota(jnp.int32, (c, PAIR), 1)
    head0 = lane < HGRN_DK
    slane = lax.broadcasted_iota(jnp.int32, (c, 2 * c), 1)
    srow = lax.broadcasted_iota(jnp.int32, (c, 2 * c), 0)
    key_pos = jnp.where(slane < c, slane, slane - c)
    brow = lax.broadcasted_iota(jnp.int32, (PAIR, PAIR), 0)
    bcol = lax.broadcasted_iota(jnp.int32, (PAIR, PAIR), 1)
    same_head = (brow < HGRN_DK) == (bcol < HGRN_DK)
    head_mean = jnp.where(same_head, 1.0 / HGRN_DK, 0.0).astype(BF16)

    lbl = lbl_ref[...]

    def lower_bound(d):
        l0 = lbl[2 * d:2 * d + 1]
        l1 = lbl[2 * d + 1:2 * d + 2]
        m = jnp.maximum(l0, l1)
        e0 = jnp.exp(l0 - m)
        e1 = jnp.exp(l1 - m)
        return e0 / (e0 + e1)

    def sweep_group(n, st, gate_ref, lb, reverse):
        rows = pl.ds(pl.multiple_of(n * rg, rg), rg)
        q = _silu(hq_ref[0, 0, rows, :])
        v16 = hi_ref[0, 0, rows, :].astype(BF16)
        b = 0.5 * (1.0 - lb)
        bt = b * jnp.tanh(0.5 * gate_ref[0, 0, rows, :])
        log2_f = jnp.log2((0.5 * (1.0 + lb)) + bt)
        k = b - bt
        tri = tri_bwd if reverse else tri_fwd
        keep = (key_pos >= srow) if reverse else (key_pos <= srow)
        lf_hi, lf_lo = _split_bf16(log2_f)
        cum = jnp.concatenate(
            [_dot(tri, lf_hi[r:r + TRI_ROWS]) + _dot(tri, lf_lo[r:r + TRI_ROWS])
             for r in range(0, rg, TRI_ROWS)], axis=0)
        zero16 = jnp.zeros((c, PAIR), BF16)
        order = range(group - 1, -1, -1) if reverse else range(group)
        sl = [slice(g * c, (g + 1) * c) for g in range(group)]
        mid = c // 2 if reverse else c // 2 - 1
        q_dec, decay, scores, u_t = {}, {}, {}, {}
        for g in order:
            cg = cum[sl[g]]
            cm = cg[mid:mid + 1]
            total = cg[0:1] if reverse else cg[c - 1:c]
            q_mid = q[sl[g]] * jnp.exp2(cg - cm)
            k_mid = k[sl[g]] * jnp.exp2(cm - cg)
            q_dec[g] = (q_mid * jnp.exp2(cm)).astype(BF16)
            k_end = (k_mid * jnp.exp2(total - cm)).astype(BF16)
            decay[g] = jnp.exp2(total)
            ki = k_mid.astype(BF16)
            kk = jnp.concatenate([jnp.where(head0, ki, zero16), jnp.where(head0, zero16, ki)], axis=0)
            scores[g] = _dot_nt(q_mid.astype(BF16), kk)
            u_t[g] = _dot_tn(v16[sl[g]], k_end)
        sts = {}
        for g in order:
            sts[g] = st.T.astype(BF16)
            st = st * decay[g] + jnp.where(same_head, u_t[g], 0.0)
        outs = [None] * group
        for g in order:
            vg = v16[sl[g]]
            vv = jnp.concatenate([jnp.where(head0, vg, zero16), jnp.where(head0, zero16, vg)], axis=0)
            p = jnp.where(keep, scores[g], 0.0).astype(BF16)
            outs[g] = _dot(p, vv) + _dot(q_dec[g], sts[g])
        return rows, jnp.concatenate(outs, axis=0), st

    st0 = jnp.zeros((PAIR, PAIR), F32)

    lb_f = lower_bound(0)

    def fwd_body(n, st):
        rows, o, st = sweep_group(n, st, hf_ref, lb_f, False)
        oacc_ref[rows, :] = o
        return st

    lax.fori_loop(0, n_groups, fwd_body, st0)

    lb_b = lower_bound(1)
    gn = gn_ref[...]

    def bwd_body(i, st):
        rows, o, st = sweep_group(n_groups - 1 - i, st, hb_ref, lb_b, True)
        o = o + oacc_ref[rows, :]
        ms = _dot((o * o).astype(BF16), head_mean)
        gate = _silu(hg_ref[0, 0, rows, :])
        y_ref[0, 0, rows, :] = (o * lax.rsqrt(ms + EPS) * gn * gate).astype(y_ref.dtype)
        return st

    lax.fori_loop(0, n_groups, bwd_body, st0)


def _hgrn_call(hq, hi, hf, hb, hg, lbl, gn, *, group):
    bsz, n_pairs, seq, _ = hq.shape
    grid = (bsz, n_pairs)
    blk = pl.BlockSpec((1, 1, seq, PAIR), lambda b, p: (b, p, 0, 0))
    return pl.pallas_call(
        functools.partial(_hgrn_kernel, n_groups=seq // (HGRN_BLOCK * group), group=group),
        grid=grid,
        in_specs=[blk, blk, blk, blk, blk,
                  pl.BlockSpec((4, PAIR), lambda b, p: (0, p)),
                  pl.BlockSpec((1, PAIR), lambda b, p: (0, p))],
        out_specs=blk,
        out_shape=jax.ShapeDtypeStruct((bsz, n_pairs, seq, PAIR), BF16),
        scratch_shapes=[pltpu.VMEM((seq, PAIR), F32)],
        compiler_params=pltpu.CompilerParams(
            dimension_semantics=("parallel", "parallel"), vmem_limit_bytes=VMEM_LIMIT),
        name="hgrn2",
    )(hq, hi, hf, hb, hg, lbl, gn)


def _attn_kernel(qt_ref, kc_ref, vt_ref, o_ref, s0_ref, s1_ref, m0_ref, m1_ref, acc_ref, *, tk, tq):
    r = pl.program_id(0)
    seq = kc_ref.shape[2]
    n_qblocks = qt_ref.shape[2] // tq

    @pl.when(r == 0)
    def _():
        s1_ref[...] = jnp.zeros_like(s1_ref)
        m1_ref[...] = jnp.zeros_like(m1_ref)

    def step(s_cur, s_prev, m_cur_ref, m_prev_ref):
        for j in range(n_qblocks):
            qs = slice(j * tq, (j + 1) * tq)
            q_t = qt_ref[0, :, qs]
            m_prev = m_prev_ref[:, qs]
            m_cur = None
            l = None
            for c in range(seq // tk):
                ks = slice(c * tk, (c + 1) * tk)
                s = _dot(kc_ref[0, 0, ks, :], q_t)
                cm = jnp.max(s, axis=0, keepdims=True)
                m_cur = cm if m_cur is None else jnp.maximum(m_cur, cm)
                p = jnp.exp2(s_prev[ks, qs] - m_prev)
                s_cur[ks, qs] = s
                ps = jnp.sum(p, axis=0, keepdims=True)
                l = ps if l is None else l + ps
                pv = _dot(vt_ref[0, :, ks], p.astype(BF16))
                if c == 0:
                    acc_ref[...] = pv
                else:
                    acc_ref[...] += pv
            m_cur_ref[:, qs] = m_cur
            o_ref[0, qs, :] = (acc_ref[...] / l).T

    @pl.when(r % 2 == 0)
    def _():
        step(s0_ref, s1_ref, m0_ref, m1_ref)

    @pl.when(r % 2 == 1)
    def _():
        step(s1_ref, s0_ref, m1_ref, m0_ref)


def _attn_call(qt, kc, vt, *, tq, tk, n_qblocks):
    bsz, _, seq = qt.shape
    tqs = tq * n_qblocks
    nq = seq // tqs
    n_prob = bsz * MLA_HEADS * nq

    def prob(r):
        r = jnp.minimum(r, n_prob - 1)
        return r // (MLA_HEADS * nq), (r // nq) % MLA_HEADS, r % nq

    def lag(r):
        return prob(jnp.maximum(r - 1, 0))

    return pl.pallas_call(
        functools.partial(_attn_kernel, tk=tk, tq=tq),
        grid=(n_prob + 1,),
        in_specs=[
            pl.BlockSpec((1, MLA_QK, tqs), lambda r: prob(r)),
            pl.BlockSpec((1, 1, seq, MLA_QK), lambda r: prob(r)[:2] + (0, 0)),
            pl.BlockSpec((1, MLA_V, seq), lambda r: lag(r)[:2] + (0,)),
        ],
        out_specs=pl.BlockSpec((1, tqs, MLA_V), lambda r: (lag(r)[0], lag(r)[2], lag(r)[1])),
        out_shape=jax.ShapeDtypeStruct((bsz, seq, MLA_WIDTH), F32),
        scratch_shapes=[pltpu.VMEM((seq, tqs), F32), pltpu.VMEM((seq, tqs), F32),
                        pltpu.VMEM((1, tqs), F32), pltpu.VMEM((1, tqs), F32),
                        pltpu.VMEM((MLA_V, tq), F32)],
        compiler_params=pltpu.CompilerParams(
            dimension_semantics=("arbitrary",), vmem_limit_bytes=VMEM_LIMIT),
        name="mla_attn",
    )(qt, kc, vt)


def _out_ffn_kernel(x_ref, ya_ref, ob_ref, gm_ref, wo_ref, g2_ref, wg_ref, wu_ref, wd_ref, gf_ref,
                    out_ref, *, n_sub):
    tm = x_ref.shape[1]
    ts = tm // n_sub
    subs = [slice(i * ts, (i + 1) * ts) for i in range(n_sub)]
    ya = [jnp.concatenate([ya_ref[0, p, r, :] for p in range(ya_ref.shape[1])], axis=1) for r in subs]
    x1 = [x_ref[0, r, :] + _dot(y, wo_ref[:HGRN_WIDTH, :])
          + _dot(_rms(ob_ref[0, r, :], gm_ref[...]).astype(BF16), wo_ref[HGRN_WIDTH:, :])
          for r, y in zip(subs, ya)]
    h2 = [_rms(v, g2_ref[...]).astype(BF16) for v in x1]
    gate = [_dot(h, wg_ref[...]) for h in h2]
    up = [_dot(h, wu_ref[...]) for h in h2]
    a = [(_silu(g) * u).astype(BF16) for g, u in zip(gate, up)]
    for r, v, ai in zip(subs, x1, a):
        out_ref[0, r, :] = _rms(v + _dot(ai, wd_ref[...]), gf_ref[...])


def _out_ffn_call(x, ya, ob, gm, wo, g2, wg, wu, wd, gf, *, tm, n_sub):
    bsz, seq, d = x.shape
    grid = (bsz, seq // tm)
    tok = lambda w: pl.BlockSpec((1, tm, w), lambda b, i: (b, i, 0))
    full = lambda a: pl.BlockSpec(a.shape, lambda b, i: (0,) * a.ndim,
                                  pipeline_mode=pl.Buffered(1))
    return pl.pallas_call(
        functools.partial(_out_ffn_kernel, n_sub=n_sub),
        grid=grid,
        in_specs=[tok(d), pl.BlockSpec((1, ya.shape[1], tm, PAIR), lambda b, i: (b, 0, i, 0)), tok(MLA_WIDTH),
                  full(gm), full(wo), full(g2),
                  full(wg), full(wu), full(wd), full(gf)],
        out_specs=tok(d),
        out_shape=jax.ShapeDtypeStruct((bsz, seq, d), x.dtype),
        compiler_params=pltpu.CompilerParams(
            dimension_semantics=("parallel", "parallel"), vmem_limit_bytes=VMEM_LIMIT),
        name="out_ffn",
    )(x, ya, ob, gm, wo, g2, wg, wu, wd, gf)


class _Tiles(NamedTuple):
    proj_rows: int
    hgrn_group: int
    attn_q: int
    attn_qblocks: int
    attn_k: int
    ffn_rows: int
    ffn_sub: int


def _tiles(seq):
    rows = min(512, seq)
    return _Tiles(proj_rows=min(2 * rows, seq), hgrn_group=min(32, seq // HGRN_BLOCK),
                  attn_q=rows, attn_qblocks=min(2, seq // rows), attn_k=rows, ffn_rows=rows, ffn_sub=2)


def _rope_tables(seq):
    inv = 1.0 / (ROPE_THETA ** (jnp.arange(0, MLA_ROPE, 2, dtype=F32) / MLA_ROPE))
    ang = jnp.arange(seq, dtype=F32)[:, None] * inv[None, :]
    return jnp.cos(ang), jnp.sin(ang)


def kernel(x, norm1_g, w_in, lb_logits, hgrn_norm_g, q_a_norm_g, w_q_b, kv_a_norm_g, w_kv_b,
           mla_norm_g, w_out, norm2_g, w_gate, w_up, w_down, final_norm_g):
    bsz, seq, d = x.shape
    assert w_in.shape[0] == 1 and lb_logits.shape[1] == 2, "single-layer trunk only"
    t = _tiles(seq)
    n_h = 5 * HGRN_WIDTH

    w_in0 = w_in[0]
    wh = w_in0[:, :n_h].astype(BF16)
    w_kr = w_in0[:, n_h + Q_LORA + KV_LORA:]
    half = MLA_ROPE // 2
    w_kr_rot = jnp.concatenate([-w_kr[:, half:], w_kr[:, :half]], axis=1)
    wc = jnp.concatenate([w_in0[:, n_h:], w_kr_rot], axis=1).astype(BF16)
    wqt = w_q_b[0].T.astype(BF16)
    w_kv = w_kv_b[0].reshape(KV_LORA, MLA_HEADS, MLA_NOPE + MLA_V)
    wkn = w_kv[:, :, :MLA_NOPE].reshape(KV_LORA, MLA_HEADS * MLA_NOPE).astype(BF16)
    wvt = w_kv[:, :, MLA_NOPE:].reshape(KV_LORA, MLA_WIDTH).T.astype(BF16)
    cos, sin = _rope_tables(seq)
    cs = jnp.concatenate([cos, cos, sin, sin], axis=1)
    row = lambda a: a.reshape(1, -1)

    hq, hi, hf, hb, hg, qt, kc, vt = _in_proj_call(
        x, row(norm1_g[0]), wh, wc, row(q_a_norm_g[0]), row(kv_a_norm_g[0]), wqt, wkn, wvt,
        cs, cos.T, sin.T, tm=t.proj_rows, scale=float(MLA_QK) ** -0.5 * math.log2(math.e))

    y_a = _hgrn_call(hq, hi, hf, hb, hg, lb_logits.reshape(4, HGRN_WIDTH), row(hgrn_norm_g[0]),
                     group=t.hgrn_group)
    o_b = _attn_call(qt, kc, vt, tq=t.attn_q, tk=t.attn_k, n_qblocks=t.attn_qblocks)

    return _out_ffn_call(
        x, y_a, o_b, row(mla_norm_g[0]), w_out[0].astype(BF16), row(norm2_g[0]),
        w_gate[0].astype(BF16), w_up[0].astype(BF16), w_down[0].astype(BF16), row(final_norm_g),
        tm=t.ffn_rows, n_sub=t.ffn_sub)
```
